```python
import math
import jax, jax.numpy as jnp
from jax import lax
import numpy as np

D_MODEL = 4096
BATCH = 1
SEQ = 8192
DEPTH = 4
DEC_BATCH = 8
DEC_SEQ = 16
PAST_LEN = 2048

CHUNK = 64
MIX_WIDTH = D_MODEL
S5_WIDTH = MIX_WIDTH // 2
S5_GROUP = 16
S5_GROUPS = S5_WIDTH // S5_GROUP
S5_STATE = 64
HG_WIDTH = MIX_WIDTH - S5_WIDTH
HG_HEADS = 16
HG_DK = 128
HG_DV = HG_WIDTH // HG_HEADS
HG_FDIM = HG_HEADS * HG_DK
IN_WIDTH = S5_WIDTH + 2 * HG_FDIM + 2 * HG_WIDTH
D_FF = -(-8 * D_MODEL // (3 * 256)) * 256
EPS = 1e-6
F_TINY = 1e-30

kernel_name = "hymba_s5_hgrn2_streaming_step"


def rms_norm(x, gain):
    xf = x.astype(jnp.float32)
    y = xf * lax.rsqrt(jnp.mean(xf * xf, axis=-1, keepdims=True) + EPS)
    return (y * gain.astype(jnp.float32)).astype(x.dtype)


def _complex_affine_combine(e1, e2):
    a1r, a1i, b1r, b1i = e1
    a2r, a2i, b2r, b2i = e2
    return (a2r * a1r - a2i * a1i,
            a2r * a1i + a2i * a1r,
            a2r * b1r - a2i * b1i + b2r,
            a2r * b1i + a2i * b1r + b2i)


def s5_group(u, h0_re, h0_im, a_re, a_im, log_dt, b_re, b_im, c_re, c_im, d, w_glu, out_gain):
    f32 = jnp.float32
    bsz, t, _ = u.shape
    uf = u.astype(f32).reshape(bsz, t, S5_GROUPS, S5_GROUP)
    a_re = a_re.astype(f32)
    a_im = a_im.astype(f32)
    dt = jnp.exp(log_dt.astype(f32))[:, None]
    mag = jnp.exp(a_re * dt)
    lam_re = mag * jnp.cos(a_im * dt)
    lam_im = mag * jnp.sin(a_im * dt)
    den = a_re * a_re + a_im * a_im
    coef_re = ((lam_re - 1.0) * a_re + lam_im * a_im) / den
    coef_im = (lam_im * a_re - (lam_re - 1.0) * a_im) / den
    b_re = b_re.astype(f32)
    b_im = b_im.astype(f32)
    bb_re = coef_re[..., None] * b_re - coef_im[..., None] * b_im
    bb_im = coef_re[..., None] * b_im + coef_im[..., None] * b_re
    bu_re = jnp.einsum('btgh,gph->btgp', uf, bb_re)
    bu_im = jnp.einsum('btgh,gph->btgp', uf, bb_im)
    h0_re = h0_re.astype(f32)
    h0_im = h0_im.astype(f32)
    bu_re = bu_re.at[:, 0].add(lam_re * h0_re - lam_im * h0_im)
    bu_im = bu_im.at[:, 0].add(lam_re * h0_im + lam_im * h0_re)
    lr = jnp.broadcast_to(lam_re, bu_re.shape)
    li = jnp.broadcast_to(lam_im, bu_im.shape)
    _, _, h_re, h_im = lax.associative_scan(_complex_affine_combine, (lr, li, bu_re, bu_im), axis=1)
    y = (jnp.einsum('btgp,ghp->btgh', h_re, c_re.astype(f32))
         - jnp.einsum('btgp,ghp->btgh', h_im, c_im.astype(f32))
         + d.astype(f32) * uf)
    y = y.reshape(bsz, t, S5_WIDTH)
    z = jax.nn.gelu(y, approximate=False)
    z = z * jax.nn.sigmoid(z @ w_glu.astype(f32))
    return rms_norm(z, out_gain).astype(u.dtype), h_re[:, -1], h_im[:, -1]


def hgrn2_group(q, fz, vi, g, s0, lb, out_gain):
    f32 = jnp.float32
    bsz, t, _ = q.shape
    qf = jax.nn.silu(q.astype(f32)).reshape(bsz, t, HG_HEADS, HG_DK)
    z = fz.astype(f32).reshape(bsz, t, HG_HEADS, HG_DK)
    lb = lb.astype(f32).reshape(HG_HEADS, HG_DK)
    f = lb + (1.0 - lb) * jax.nn.sigmoid(z)
    logf = jnp.log(jnp.maximum(f, F_TINY))
    kf = (1.0 - lb) * jax.nn.sigmoid(-z)
    vf = vi.astype(f32).reshape(bsz, t, HG_HEADS, HG_DV)
    c = min(CHUNK, t)
    n = t // c

    def to_chunks(a):
        return a.reshape(bsz, n, c, HG_HEADS, a.shape[-1]).transpose(1, 0, 3, 2, 4)

    mask = jnp.tril(jnp.ones((c, c), dtype=bool))[:, :, None]

    def step(s, inp):
        qc, lc, kc, vc = inp
        bc = jnp.cumsum(lc, axis=2)
        blast = bc[:, :, -1:, :]
        o_inter = jnp.einsum('bhtk,bhkv->bhtv', qc * jnp.exp(bc), s)
        diff = bc[:, :, :, None, :] - bc[:, :, None, :, :]
        decay = jnp.where(mask, jnp.exp(jnp.minimum(diff, 0.0)), 0.0)
        att = jnp.einsum('bhtk,bhtsk,bhsk->bhts', qc, decay, kc)
        o = o_inter + jnp.einsum('bhts,bhsv->bhtv', att, vc)
        s_new = (jnp.exp(blast[:, :, 0, :])[..., None] * s
                 + jnp.einsum('bhsk,bhsv->bhkv', kc * jnp.exp(blast - bc), vc))
        return s_new, o

    s_fin, o = lax.scan(step, s0.astype(f32),
                        (to_chunks(qf), to_chunks(logf), to_chunks(kf), to_chunks(vf)))
    o = o.transpose(1, 0, 3, 2, 4).reshape(bsz, t, HG_HEADS, HG_DV)
    gate = jax.nn.silu(g.astype(f32).reshape(bsz, t, HG_HEADS, HG_DV))
    out = (rms_norm(o, out_gain) * gate).reshape(bsz, t, HG_WIDTH)
    return out.astype(q.dtype), s_fin


def trunk_layer(x, h0_re, h0_im, s0, lb, w_in, w_out, norm_mix, norm_ffn,
                a_re, a_im, log_dt, b_re, b_im, c_re, c_im, d, w_glu, s5_gain,
                hg_gain, w_gate, w_up, w_down):
    h = rms_norm(x, norm_mix)
    proj = h @ w_in
    o1 = S5_WIDTH
    o2 = o1 + HG_FDIM
    o3 = o2 + HG_FDIM
    o4 = o3 + HG_WIDTH
    u, q, fz, vi, g = proj[..., :o1], proj[..., o1:o2], proj[..., o2:o3], proj[..., o3:o4], proj[..., o4:]
    s5_out, h_re, h_im = s5_group(u, h0_re, h0_im, a_re, a_im, log_dt, b_re, b_im,
                                  c_re, c_im, d, w_glu, s5_gain)
    hg_out, s_new = hgrn2_group(q, fz, vi, g, s0, lb, hg_gain)
    x = x + jnp.concatenate([s5_out, hg_out], axis=-1) @ w_out
    h = rms_norm(x, norm_ffn)
    x = x + (jax.nn.silu(h @ w_gate) * (h @ w_up)) @ w_down
    return x, h_re, h_im, s_new


def setup_inputs(seed: int = 0) -> dict:
    key = jax.random.key(seed)
    ks = jax.random.split(key, 24)
    f32 = jnp.float32
    nrm = lambda k, shape, scale: scale * jax.random.normal(k, shape, f32)
    return {
        'x_prompt': nrm(ks[0], (BATCH, SEQ, D_MODEL), 1.0),
        'x_sample': nrm(ks[1], (DEC_BATCH, DEC_SEQ, D_MODEL), 1.0),
        'state_s5_re': nrm(ks[2], (DEPTH, DEC_BATCH, S5_GROUPS, S5_STATE), 0.5),
        'state_s5_im': nrm(ks[3], (DEPTH, DEC_BATCH, S5_GROUPS, S5_STATE), 0.5),
        'state_hgrn': nrm(ks[4], (DEPTH, DEC_BATCH, HG_HEADS, HG_DK, HG_DV), 0.5),
        'w_in': nrm(ks[5], (DEPTH, D_MODEL, IN_WIDTH), D_MODEL ** -0.5),
        'w_out': nrm(ks[6], (DEPTH, MIX_WIDTH, D_MODEL), MIX_WIDTH ** -0.5),
        'norm_mix': 1.0 + nrm(ks[7], (DEPTH, D_MODEL), 0.02),
        'norm_ffn': 1.0 + nrm(ks[8], (DEPTH, D_MODEL), 0.02),
        'norm_final': 1.0 + nrm(ks[9], (D_MODEL,), 0.02),
        's5_a_re': -0.5 + nrm(ks[10], (DEPTH, S5_GROUPS, S5_STATE), 0.01),
        's5_a_im': jnp.pi * jnp.arange(S5_STATE, dtype=f32) + nrm(ks[11], (DEPTH, S5_GROUPS, S5_STATE), 0.01),
        's5_log_dt': jax.random.uniform(ks[12], (DEPTH, S5_GROUPS), f32, math.log(1e-3), math.log(1e-1)),
        's5_b_re': nrm(ks[13], (DEPTH, S5_GROUPS, S5_STATE, S5_GROUP), (2 * S5_GROUP) ** -0.5),
        's5_b_im': nrm(ks[14], (DEPTH, S5_GROUPS, S5_STATE, S5_GROUP), (2 * S5_GROUP) ** -0.5),
        's5_c_re': nrm(ks[15], (DEPTH, S5_GROUPS, S5_GROUP, S5_STATE), (2 * S5_STATE) ** -0.5),
        's5_c_im': nrm(ks[16], (DEPTH, S5_GROUPS, S5_GROUP, S5_STATE), (2 * S5_STATE) ** -0.5),
        's5_d': nrm(ks[17], (DEPTH, S5_GROUPS, S5_GROUP), 1.0),
        's5_w_glu': nrm(ks[18], (DEPTH, S5_WIDTH, S5_WIDTH), S5_WIDTH ** -0.5),
        's5_out_norm': 1.0 + nrm(ks[19], (DEPTH, S5_WIDTH), 0.02),
        'hg_lower_bounds': nrm(ks[20], (DEPTH, HG_FDIM), 0.5),
        'hg_out_norm': 1.0 + nrm(ks[21], (DEPTH, HG_DV), 0.02),
        'ffn_w_gate': nrm(ks[22], (DEPTH, D_MODEL, D_FF), D_MODEL ** -0.5),
        'ffn_w_up': nrm(jax.random.fold_in(ks[22], 1), (DEPTH, D_MODEL, D_FF), D_MODEL ** -0.5),
        'ffn_w_down': nrm(ks[23], (DEPTH, D_FF, D_MODEL), D_FF ** -0.5),
    }


def reference(x_prompt, x_sample, state_s5_re, state_s5_im, state_hgrn,
              w_in, w_out, norm_mix, norm_ffn, norm_final,
              s5_a_re, s5_a_im, s5_log_dt, s5_b_re, s5_b_im, s5_c_re, s5_c_im, s5_d,
              s5_w_glu, s5_out_norm, hg_lower_bounds, hg_out_norm,
              ffn_w_gate, ffn_w_up, ffn_w_down):
    f32 = jnp.float32
    sm = jax.nn.softmax(hg_lower_bounds.astype(f32), axis=0)
    lbs = jnp.cumsum(sm, axis=0) - sm[0]
    nb = x_prompt.shape[0]
    zero_s5 = jnp.zeros((nb, S5_GROUPS, S5_STATE), f32)
    zero_hg = jnp.zeros((nb, HG_HEADS, HG_DK, HG_DV), f32)
    yp, ys = x_prompt, x_sample
    p_re, p_im, p_hg, s_re, s_im, s_hg = [], [], [], [], [], []
    for l in range(DEPTH):
        lw = (lbs[l], w_in[l], w_out[l], norm_mix[l], norm_ffn[l],
              s5_a_re[l], s5_a_im[l], s5_log_dt[l], s5_b_re[l], s5_b_im[l],
              s5_c_re[l], s5_c_im[l], s5_d[l], s5_w_glu[l], s5_out_norm[l],
              hg_out_norm[l], ffn_w_gate[l], ffn_w_up[l], ffn_w_down[l])
        yp, hr, hi, hs = trunk_layer(yp, zero_s5, zero_s5, zero_hg, *lw)
        p_re.append(hr)
        p_im.append(hi)
        p_hg.append(hs)
        ys, hr, hi, hs = trunk_layer(ys, state_s5_re[l], state_s5_im[l], state_hgrn[l], *lw)
        s_re.append(hr)
        s_im.append(hi)
        s_hg.append(hs)
    y_prompt = rms_norm(yp, norm_final)
    y_sample = rms_norm(ys, norm_final)
    new_s5_re_prompt = jnp.stack(p_re)
    new_s5_im_prompt = jnp.stack(p_im)
    new_hgrn_prompt = jnp.stack(p_hg)
    new_s5_re_sample = jnp.stack(s_re)
    new_s5_im_sample = jnp.stack(s_im)
    new_hgrn_sample = jnp.stack(s_hg)
    return (y_prompt, y_sample, new_s5_re_prompt, new_s5_im_prompt, new_hgrn_prompt,
            new_s5_re_sample, new_s5_im_sample, new_hgrn_sample)
```

```python
import functools
import math

import jax
import jax.numpy as jnp
from jax import lax
from jax.experimental import pallas as pl
from jax.experimental.pallas import tpu as pltpu

f32 = jnp.float32
bf16 = jnp.bfloat16

EPS = 1e-6
F_TINY = 1e-30
HG_CHUNK = 64
SUBLANES = 8
LANES = 128
S5_SET_GROUPS = 16
VMEM_LIMIT = 56 * 1024 * 1024


def _cparams(sem):
    return pltpu.CompilerParams(dimension_semantics=sem, vmem_limit_bytes=VMEM_LIMIT)


def _rms(x, gain):
    return x * lax.rsqrt(jnp.mean(x * x, axis=-1, keepdims=True) + EPS) * gain


def _prep_kernel(are_ref, aim_ref, ldt_ref, arer_ref, aimr_ref, ldtr_ref, bre_ref, bim_ref, lb_ref,
                 pwr_ref, pwi_ref, bbr_ref, bbi_ref, lbs_ref):
    a_re = are_ref[...]
    a_im = aim_ref[...]
    dt = jnp.exp(ldt_ref[...])
    for k in range(1, SUBLANES + 1):
        mag = jnp.exp(a_re * dt * float(k))
        ang = a_im * dt * float(k)
        pwr_ref[:, k - 1, :] = mag * jnp.cos(ang)
        pwi_ref[:, k - 1, :] = mag * jnp.sin(ang)
    a_re = arer_ref[...]
    a_im = aimr_ref[...]
    dt = jnp.exp(ldtr_ref[...])
    mag = jnp.exp(a_re * dt)
    lam_re = mag * jnp.cos(a_im * dt)
    lam_im = mag * jnp.sin(a_im * dt)
    den = a_re * a_re + a_im * a_im
    coef_re = ((lam_re - 1.0) * a_re + lam_im * a_im) / den
    coef_im = (lam_im * a_re - (lam_re - 1.0) * a_im) / den
    b_re = bre_ref[...]
    b_im = bim_ref[...]
    bbr_ref[...] = coef_re * b_re - coef_im * b_im
    bbi_ref[...] = coef_re * b_im + coef_im * b_re
    lb = lb_ref[...]
    e = jnp.exp(lb - jnp.max(lb, axis=0, keepdims=True))
    sm = e / jnp.sum(e, axis=0, keepdims=True)
    acc = jnp.zeros_like(sm[0:1])
    for l in range(lb.shape[0]):
        acc = acc + sm[l:l + 1]
        lbs_ref[l:l + 1, :] = acc - sm[0:1]


def _prep(a_re, a_im, log_dt, b_re, b_im, lower_bounds):
    L, G, P = a_re.shape
    H = b_re.shape[-1]
    are = a_re.reshape(L, G * P)
    aim = a_im.reshape(L, G * P)
    ldt = jnp.repeat(log_dt, P, axis=-1)
    arer = jnp.repeat(a_re.reshape(L * G, P), H, axis=-1)
    aimr = jnp.repeat(a_im.reshape(L * G, P), H, axis=-1)
    ldtr = jnp.broadcast_to(log_dt.reshape(L * G, 1), (L * G, P * H))
    out_shape = (
        jax.ShapeDtypeStruct((L, SUBLANES, G * P), f32),
        jax.ShapeDtypeStruct((L, SUBLANES, G * P), f32),
        jax.ShapeDtypeStruct((L * G, P * H), f32),
        jax.ShapeDtypeStruct((L * G, P * H), f32),
        jax.ShapeDtypeStruct(lower_bounds.shape, f32),
    )
    pwr, pwi, bbr, bbi, lbs = pl.pallas_call(_prep_kernel, out_shape=out_shape, name="param_prep")(
        are, aim, ldt, arer, aimr, ldtr, b_re.reshape(L * G, P * H), b_im.reshape(L * G, P * H), lower_bounds)
    return pwr, pwi, bbr.reshape(L, G, P, H), bbi.reshape(L, G, P, H), lbs


def _norm_matmul_kernel(x_ref, g_ref, w_ref, o_ref, h_ref):
    @pl.when(pl.program_id(1) == 0)
    def _():
        h_ref[...] = _rms(x_ref[...], g_ref[...]).astype(bf16)

    o_ref[...] = jnp.dot(h_ref[...], w_ref[...], preferred_element_type=f32)


def _norm_matmul(x, gain, w, tm, tn):
    M, D = x.shape
    N = w.shape[1]
    return pl.pallas_call(
        _norm_matmul_kernel,
        grid=(M // tm, N // tn),
        in_specs=[
            pl.BlockSpec((tm, D), lambda i, j: (i, 0)),
            pl.BlockSpec((1, D), lambda i, j: (0, 0)),
            pl.BlockSpec((D, tn), lambda i, j: (0, j)),
        ],
        out_specs=pl.BlockSpec((tm, tn), lambda i, j: (i, j)),
        out_shape=jax.ShapeDtypeStruct((M, N), f32),
        scratch_shapes=[pltpu.VMEM((tm, D), bf16)],
        compiler_params=_cparams(("parallel", "arbitrary")),
        name="norm_in_proj",
    )(x, gain.reshape(1, D), w)


def _s5_kernel(u_ref, bexp_ref, cexp_ref, pwr_ref, pwi_ref, d_ref, h0_ref, wglu_ref, gain_ref,
               out_ref, hfin_ref, bu_ref, tab_ref, z_ref, *, n_sets, lane_chunk):
    i = pl.program_id(1)
    s = pl.program_id(2)
    tt = u_ref.shape[0]
    W = pwr_ref.shape[-1]

    @pl.when(i == 0)
    def _():
        hfin_ref[s] = h0_ref[...]

    row = lax.broadcasted_iota(jnp.int32, (SUBLANES, W), 0)
    pwr = pwr_ref[...]
    pwi = pwi_ref[...]
    for n, d in enumerate((1, 2, 4)):
        keep = row >= d
        tab_ref[2 * n] = jnp.where(keep, jnp.broadcast_to(pwr[d - 1:d], (SUBLANES, W)), 0.0)
        tab_ref[2 * n + 1] = jnp.where(keep, jnp.broadcast_to(pwi[d - 1:d], (SUBLANES, W)), 0.0)
    tab_ref[6] = pwr
    tab_ref[7] = pwi

    u = u_ref[...]
    bu_ref[...] = jnp.dot(u.astype(bf16), bexp_ref[...], preferred_element_type=f32)

    n_chunks = W // lane_chunk

    def block(r, carry):
        row0 = pl.multiple_of(r * SUBLANES, SUBLANES)
        new = []
        for c in range(n_chunks):
            lo = c * lane_chunk
            xr = bu_ref[pl.ds(row0, SUBLANES), lo:lo + lane_chunk]
            xi = bu_ref[pl.ds(row0, SUBLANES), W + lo:W + lo + lane_chunk]
            for n, d in enumerate((1, 2, 4)):
                tr = tab_ref[2 * n, :, lo:lo + lane_chunk]
                ti = tab_ref[2 * n + 1, :, lo:lo + lane_chunk]
                sr = pltpu.roll(xr, d, axis=0)
                si = pltpu.roll(xi, d, axis=0)
                xr, xi = xr + (tr * sr - ti * si), xi + (tr * si + ti * sr)
            pr = tab_ref[6, :, lo:lo + lane_chunk]
            pi_ = tab_ref[7, :, lo:lo + lane_chunk]
            cr = carry[2 * c]
            ci = carry[2 * c + 1]
            xr, xi = xr + (pr * cr - pi_ * ci), xi + (pr * ci + pi_ * cr)
            bu_ref[pl.ds(row0, SUBLANES), lo:lo + lane_chunk] = xr
            bu_ref[pl.ds(row0, SUBLANES), W + lo:W + lo + lane_chunk] = xi
            new.append(xr[SUBLANES - 1:SUBLANES])
            new.append(xi[SUBLANES - 1:SUBLANES])
        return tuple(new)

    c0 = hfin_ref[s]
    init = []
    for c in range(n_chunks):
        lo = c * lane_chunk
        init.append(c0[:, lo:lo + lane_chunk])
        init.append(c0[:, W + lo:W + lo + lane_chunk])
    fin = lax.fori_loop(0, tt // SUBLANES, block, tuple(init))
    cfin = jnp.concatenate([fin[2 * c] for c in range(n_chunks)] + [fin[2 * c + 1] for c in range(n_chunks)], axis=-1)
    hfin_ref[s] = cfin

    y = jnp.dot(bu_ref[...].astype(bf16), cexp_ref[...], preferred_element_type=f32) + d_ref[...] * u
    z_ref[s] = 0.5 * y * (1.0 + lax.erf(y * (1.0 / math.sqrt(2.0))))

    @pl.when(s == n_sets - 1)
    def _():
        z = jnp.concatenate([z_ref[k] for k in range(n_sets)], axis=-1)
        gl = jnp.dot(z.astype(bf16), wglu_ref[...], preferred_element_type=f32)
        out_ref[...] = _rms(z * jax.nn.sigmoid(gl), gain_ref[...]).astype(out_ref.dtype)


def _s5_group(proj, bexp, cexp, pwr, pwi, d, h0, wglu, gain, batch, seq, tt):
    n_sets, cw, w2 = bexp.shape
    W = w2 // 2
    nt = seq // tt
    s5w = n_sets * cw
    kern = functools.partial(_s5_kernel, n_sets=n_sets, lane_chunk=min(256, W))
    return pl.pallas_call(
        kern,
        grid=(batch, nt, n_sets),
        in_specs=[
            pl.BlockSpec((tt, cw), lambda b, i, s: (b * nt + i, s)),
            pl.BlockSpec((None, cw, w2), lambda b, i, s: (s, 0, 0)),
            pl.BlockSpec((None, w2, cw), lambda b, i, s: (s, 0, 0)),
            pl.BlockSpec((None, SUBLANES, W), lambda b, i, s: (s, 0, 0)),
            pl.BlockSpec((None, SUBLANES, W), lambda b, i, s: (s, 0, 0)),
            pl.BlockSpec((None, 1, cw), lambda b, i, s: (s, 0, 0)),
            pl.BlockSpec((None, None, 1, w2), lambda b, i, s: (b, s, 0, 0)),
            pl.BlockSpec((s5w, s5w), lambda b, i, s: (0, 0)),
            pl.BlockSpec((1, s5w), lambda b, i, s: (0, 0)),
        ],
        out_specs=[
            pl.BlockSpec((tt, s5w), lambda b, i, s: (b * nt + i, 0)),
            pl.BlockSpec((None, n_sets, 1, w2), lambda b, i, s: (b, 0, 0, 0)),
        ],
        out_shape=[
            jax.ShapeDtypeStruct((batch * seq, s5w), bf16),
            jax.ShapeDtypeStruct((batch, n_sets, 1, w2), f32),
        ],
        scratch_shapes=[
            pltpu.VMEM((tt, w2), f32),
            pltpu.VMEM((8, SUBLANES, W), f32),
            pltpu.VMEM((n_sets, tt, cw), f32),
        ],
        compiler_params=_cparams(("arbitrary", "arbitrary", "arbitrary")),
        name="s5_group",
    )(proj, bexp, cexp, pwr, pwi, d, h0, wglu, gain)


def _cumsum_rows(x):
    C = x.shape[0]
    nb = C // SUBLANES
    x3 = x.reshape(nb, SUBLANES, x.shape[1])
    row = lax.broadcasted_iota(jnp.int32, x3.shape, 1)
    for d in (1, 2, 4):
        x3 = x3 + jnp.where(row >= d, pltpu.roll(x3, d, axis=1), 0.0)
    outs = [x3[0:1]]
    acc = x3[0:1, SUBLANES - 1:SUBLANES, :]
    for b in range(1, nb):
        outs.append(x3[b:b + 1] + acc)
        acc = acc + x3[b:b + 1, SUBLANES - 1:SUBLANES, :]
    return jnp.concatenate(outs, axis=0).reshape(C, x.shape[1])


def _dot_nt(a, b):
    return lax.dot_general(a, b, (((1,), (1,)), ((), ())), preferred_element_type=f32)


def _dot_tn(a, b):
    return lax.dot_general(a, b, (((0,), (0,)), ((), ())), preferred_element_type=f32)


def _hgrn_chunk(q, z, v, g, lb, gain, st):
    C, K = q.shape
    nb = C // SUBLANES
    qf = q * jax.nn.sigmoid(q)
    e = jnp.exp(-jnp.abs(z))
    r = 1.0 / (1.0 + e)
    er = e * r
    pos = z >= 0.0
    om = 1.0 - lb
    f = lb + om * jnp.where(pos, r, er)
    logf = jnp.log(jnp.maximum(f, F_TINY))
    kf = om * jnp.where(pos, er, r)
    bc = _cumsum_rows(logf)
    blast = bc[C - 1:C, :]
    vb = v.astype(bf16)

    o = _dot_nt((qf * jnp.exp(bc)).astype(bf16), st.astype(bf16))
    st_new = st * jnp.exp(blast) + _dot_tn(vb, (kf * jnp.exp(blast - bc)).astype(bf16))

    row = lax.broadcasted_iota(jnp.int32, (C, K), 0)
    trow = lax.broadcasted_iota(jnp.int32, (C, C), 0)
    tcol = lax.broadcasted_iota(jnp.int32, (C, C), 1)
    att = None
    m = SUBLANES
    while m < C:
        nblk = C // (2 * m)
        bref = bc.reshape(nblk, 2 * m, K)[:, m - 1:m, :]
        bref = jnp.broadcast_to(bref, (nblk, 2 * m, K)).reshape(C, K)
        dec = jnp.exp(-jnp.abs(bc - bref))
        upper = (row // m) % 2 == 1
        qm = jnp.where(upper, qf * dec, 0.0).astype(bf16)
        km = jnp.where(upper, 0.0, kf * dec).astype(bf16)
        a = _dot_nt(qm, km)
        if nblk > 1:
            a = jnp.where(trow // (2 * m) == tcol // (2 * m), a, 0.0)
        att = a if att is None else att + a
        m *= 2
    if att is not None:
        o = o + jnp.dot(att.astype(bf16), vb, preferred_element_type=f32)

    q3 = qf.reshape(nb, SUBLANES, K)
    k3 = kf.reshape(nb, SUBLANES, K)
    b3 = bc.reshape(nb, SUBLANES, K)
    v3 = v.reshape(nb, SUBLANES, K)
    row3 = lax.broadcasted_iota(jnp.int32, (nb, SUBLANES, K), 1)
    od = jnp.sum(q3 * k3, axis=-1, keepdims=True) * v3
    for j in range(1, SUBLANES):
        valid = row3 >= j
        dj = jnp.exp(jnp.where(valid, b3 - pltpu.roll(b3, j, axis=1), 0.0))
        p = jnp.where(valid, q3 * pltpu.roll(k3, j, axis=1) * dj, 0.0)
        od = od + jnp.sum(p, axis=-1, keepdims=True) * pltpu.roll(v3, j, axis=1)
    o = o + od.reshape(C, K)

    out = _rms(o, gain) * (g * jax.nn.sigmoid(g))
    return out, st_new


def _hgrn_kernel(q_ref, z_ref, v_ref, g_ref, lb_ref, gain_ref, s0_ref, out_ref, sfin_ref, st_ref, *, chunk):
    i = pl.program_id(2)
    tt = q_ref.shape[0]

    @pl.when(i == 0)
    def _():
        st_ref[...] = s0_ref[...].T

    lb = lb_ref[...]
    gain = gain_ref[...]

    def body(c, carry):
        row0 = pl.multiple_of(c * chunk, chunk)
        sl = pl.ds(row0, chunk)
        out, st_new = _hgrn_chunk(q_ref[sl, :], z_ref[sl, :], v_ref[sl, :], g_ref[sl, :], lb, gain, st_ref[...])
        st_ref[...] = st_new
        out_ref[sl, :] = out.astype(out_ref.dtype)
        return carry

    lax.fori_loop(0, tt // chunk, body, 0)

    @pl.when(i == pl.num_programs(2) - 1)
    def _():
        sfin_ref[...] = st_ref[...].T


def _hgrn_group(proj, lbs, gain, s0, batch, seq, tt, col0):
    _, H, K, V = s0.shape
    nt = seq // tt
    chunk = min(HG_CHUNK, seq)
    kern = functools.partial(_hgrn_kernel, chunk=chunk)

    def col_spec(k):
        return pl.BlockSpec((tt, K), lambda b, h, i, k=k: (b * nt + i, col0 + k * H + h))

    return pl.pallas_call(
        kern,
        grid=(batch, H, nt),
        in_specs=[
            col_spec(0), col_spec(1), col_spec(2), col_spec(3),
            pl.BlockSpec((1, K), lambda b, h, i: (0, h)),
            pl.BlockSpec((1, V), lambda b, h, i: (0, 0)),
            pl.BlockSpec((None, None, K, V), lambda b, h, i: (b, h, 0, 0)),
        ],
        out_specs=[
            pl.BlockSpec((tt, V), lambda b, h, i: (b * nt + i, h)),
            pl.BlockSpec((None, None, K, V), lambda b, h, i: (b, h, 0, 0)),
        ],
        out_shape=[
            jax.ShapeDtypeStruct((batch * seq, H * V), bf16),
            jax.ShapeDtypeStruct((batch, H, K, V), f32),
        ],
        scratch_shapes=[pltpu.VMEM((V, K), f32)],
        compiler_params=_cparams(("arbitrary", "arbitrary", "arbitrary")),
        name="hgrn2_group",
    )(proj, proj, proj, proj, lbs, gain, s0)


def _out_proj_kernel(x_ref, a_ref, b_ref, wa_ref, wb_ref, o_ref):
    acc = jnp.dot(a_ref[...], wa_ref[...], preferred_element_type=f32)
    acc = acc + jnp.dot(b_ref[...], wb_ref[...], preferred_element_type=f32)
    o_ref[...] = x_ref[...] + acc


def _out_proj(x, a, b, w, tm, tn):
    M, D = x.shape
    ka = a.shape[1]
    kb = b.shape[1]
    return pl.pallas_call(
        _out_proj_kernel,
        grid=(M // tm, D // tn),
        in_specs=[
            pl.BlockSpec((tm, tn), lambda i, j: (i, j)),
            pl.BlockSpec((tm, ka), lambda i, j: (i, 0)),
            pl.BlockSpec((tm, kb), lambda i, j: (i, 0)),
            pl.BlockSpec((ka, tn), lambda i, j: (0, j)),
            pl.BlockSpec((kb, tn), lambda i, j: (1, j)),
        ],
        out_specs=pl.BlockSpec((tm, tn), lambda i, j: (i, j)),
        out_shape=jax.ShapeDtypeStruct((M, D), f32),
        compiler_params=_cparams(("parallel", "arbitrary")),
        name="out_proj",
    )(x, a, b, w, w)


def _ffn_kernel(x_ref, g_ref, wg_ref, wu_ref, wd_ref, gf_ref, o_ref, h_ref, *, final_norm):
    j = pl.program_id(1)

    @pl.when(j == 0)
    def _():
        x = x_ref[...]
        h_ref[...] = _rms(x, g_ref[...]).astype(bf16)
        o_ref[...] = x

    h = h_ref[...]
    gate = jnp.dot(h, wg_ref[...], preferred_element_type=f32)
    up = jnp.dot(h, wu_ref[...], preferred_element_type=f32)
    act = (gate * jax.nn.sigmoid(gate) * up).astype(bf16)
    o_ref[...] += jnp.dot(act, wd_ref[...], preferred_element_type=f32)

    if final_norm:
        @pl.when(j == pl.num_programs(1) - 1)
        def _():
            o_ref[...] = _rms(o_ref[...], gf_ref[...])


def _ffn(x, gain, wg, wu, wd, gain_final, final_norm, tm, tf):
    M, D = x.shape
    F = wg.shape[1]
    kern = functools.partial(_ffn_kernel, final_norm=final_norm)
    return pl.pallas_call(
        kern,
        grid=(M // tm, F // tf),
        in_specs=[
            pl.BlockSpec((tm, D), lambda i, j: (i, 0), pipeline_mode=pl.Buffered(1)),
            pl.BlockSpec((1, D), lambda i, j: (0, 0)),
            pl.BlockSpec((D, tf), lambda i, j: (0, j)),
            pl.BlockSpec((D, tf), lambda i, j: (0, j)),
            pl.BlockSpec((tf, D), lambda i, j: (j, 0)),
            pl.BlockSpec((1, D), lambda i, j: (0, 0)),
        ],
        out_specs=pl.BlockSpec((tm, D), lambda i, j: (i, 0)),
        out_shape=jax.ShapeDtypeStruct((M, D), f32),
        scratch_shapes=[pltpu.VMEM((tm, D), bf16)],
        compiler_params=_cparams(("parallel", "arbitrary")),
        name="swiglu_ffn",
    )(x, gain.reshape(1, D), wg, wu, wd, gain_final.reshape(1, D))


def _expand_s5_weights(bb_re, bb_im, c_re, c_im):
    L, G, P, H = bb_re.shape
    ns = G // S5_SET_GROUPS
    eye = jnp.eye(S5_SET_GROUPS, dtype=f32)

    def expand_b(bb):
        t = bb.reshape(L, ns, S5_SET_GROUPS, P, H)
        return jnp.einsum('lsgph,gk->lsghkp', t, eye).reshape(L, ns, S5_SET_GROUPS * H, S5_SET_GROUPS * P)

    def expand_c(c):
        t = c.reshape(L, ns, S5_SET_GROUPS, H, P)
        return jnp.einsum('lsghp,gk->lskpgh', t, eye).reshape(L, ns, S5_SET_GROUPS * P, S5_SET_GROUPS * H)

    bexp = jnp.concatenate([expand_b(bb_re), expand_b(bb_im)], axis=-1).astype(bf16)
    cexp = jnp.concatenate([expand_c(c_re), -expand_c(c_im)], axis=-2).astype(bf16)
    return bexp, cexp


def kernel(x_prompt, x_sample, state_s5_re, state_s5_im, state_hgrn, w_in, w_out, norm_mix, norm_ffn, norm_final, s5_a_re, s5_a_im, s5_log_dt, s5_b_re, s5_b_im, s5_c_re, s5_c_im, s5_d, s5_w_glu, s5_out_norm, hg_lower_bounds, hg_out_norm, ffn_w_gate, ffn_w_up, ffn_w_down):
    L, G, P = s5_a_re.shape
    Hs = s5_b_re.shape[-1]
    _, _, HH, HK, HV = state_hgrn.shape
    PB, PT, D = x_prompt.shape
    SB, ST, _ = x_sample.shape
    ns = G // S5_SET_GROUPS
    W = S5_SET_GROUPS * P
    s5w = G * Hs

    pwr, pwi, bb_re, bb_im, lbs = _prep(s5_a_re, s5_a_im, s5_log_dt, s5_b_re, s5_b_im, hg_lower_bounds)
    bexp, cexp = _expand_s5_weights(bb_re, bb_im, s5_c_re, s5_c_im)
    pwr = pwr.reshape(L, SUBLANES, ns, W).transpose(0, 2, 1, 3)
    pwi = pwi.reshape(L, SUBLANES, ns, W).transpose(0, 2, 1, 3)
    d_sets = s5_d.reshape(L, ns, 1, S5_SET_GROUPS * Hs)

    w_in_b = w_in.astype(bf16)
    w_out_b = w_out.astype(bf16)
    w_glu_b = s5_w_glu.astype(bf16)
    wg_b = ffn_w_gate.astype(bf16)
    wu_b = ffn_w_up.astype(bf16)
    wd_b = ffn_w_down.astype(bf16)

    def s5_state(re, im, batch):
        return jnp.concatenate([re.reshape(batch, ns, 1, W), im.reshape(batch, ns, 1, W)], axis=-1)

    zero_s5 = jnp.zeros((PB, ns, 1, 2 * W), f32)
    zero_hg = jnp.zeros((PB, HH, HK, HV), f32)

    def layer(x, l, h0, s0, batch, seq, tm, tt_s5, tt_hg):
        M = batch * seq
        x2 = x.reshape(M, D)
        proj = _norm_matmul(x2, norm_mix[l], w_in_b[l], tm, 1024)
        s5_out, hfin = _s5_group(proj, bexp[l], cexp[l], pwr[l], pwi[l], d_sets[l], h0, w_glu_b[l],
                                 s5_out_norm[l].reshape(1, s5w), batch, seq, tt_s5)
        hg_out, sfin = _hgrn_group(proj, lbs[l].reshape(1, HH * HK), hg_out_norm[l].reshape(1, HV), s0,
                                   batch, seq, tt_hg, s5w // LANES)
        x2 = _out_proj(x2, s5_out, hg_out, w_out_b[l], tm, 1024)
        x2 = _ffn(x2, norm_ffn[l], wg_b[l], wu_b[l], wd_b[l], norm_final, l == L - 1, tm, 256)
        h_re = hfin[..., :W].reshape(batch, G, P)
        h_im = hfin[..., W:].reshape(batch, G, P)
        return x2.reshape(batch, seq, D), h_re, h_im, sfin

    yp, ys = x_prompt, x_sample
    p_re, p_im, p_hg, s_re, s_im, s_hg = [], [], [], [], [], []
    for l in range(L):
        yp, hr, hi, hs = layer(yp, l, zero_s5, zero_hg, PB, PT, 512, 512, 512)
        p_re.append(hr)
        p_im.append(hi)
        p_hg.append(hs)
        ys, hr, hi, hs = layer(ys, l, s5_state(state_s5_re[l], state_s5_im[l], SB), state_hgrn[l],
                               SB, ST, SB * ST, ST, ST)
        s_re.append(hr)
        s_im.append(hi)
        s_hg.append(hs)
    return (yp, ys, jnp.stack(p_re), jnp.stack(p_im), jnp.stack(p_hg),
            jnp.stack(s_re), jnp.stack(s_im), jnp.stack(s_hg))
```

```python
import functools
import math

import jax
import jax.numpy as jnp
import numpy as np
from jax import lax
from jax.experimental import pallas as pl
from jax.experimental.pallas import tpu as pltpu

f32 = jnp.float32
bf16 = jnp.bfloat16

EPS = 1e-6
F_TINY = 1e-30
HG_CHUNK = 64
SUBLANES = 8
LANES = 128
S5_SET_GROUPS = 16
HG_HEADS_PER_STEP = 4
VMEM_LIMIT = 56 * 1024 * 1024
ROW_TILE = 640
SEQ_TILE = 512
ROW_BLOCK = 64
COL_PANEL = 512


def _cparams(sem):
    return pltpu.CompilerParams(dimension_semantics=sem, vmem_limit_bytes=VMEM_LIMIT)


def _rms(x, gain):
    return x * lax.rsqrt(jnp.mean(x * x, axis=-1, keepdims=True) + EPS) * gain


def _prep_kernel(are_ref, aim_ref, ldt_ref, arer_ref, aimr_ref, ldtr_ref, bre_ref, bim_ref, lb_ref,
                 pwr_ref, pwi_ref, bbr_ref, bbi_ref, lbs_ref):
    a_re = are_ref[...]
    a_im = aim_ref[...]
    dt = jnp.exp(ldt_ref[...])
    for k in range(1, SUBLANES + 1):
        mag = jnp.exp(a_re * dt * float(k))
        ang = a_im * dt * float(k)
        pwr_ref[:, k - 1, :] = mag * jnp.cos(ang)
        pwi_ref[:, k - 1, :] = mag * jnp.sin(ang)
    a_re = arer_ref[...]
    a_im = aimr_ref[...]
    dt = jnp.exp(ldtr_ref[...])
    mag = jnp.exp(a_re * dt)
    lam_re = mag * jnp.cos(a_im * dt)
    lam_im = mag * jnp.sin(a_im * dt)
    den = a_re * a_re + a_im * a_im
    coef_re = ((lam_re - 1.0) * a_re + lam_im * a_im) / den
    coef_im = (lam_im * a_re - (lam_re - 1.0) * a_im) / den
    b_re = bre_ref[...]
    b_im = bim_ref[...]
    bbr_ref[...] = coef_re * b_re - coef_im * b_im
    bbi_ref[...] = coef_re * b_im + coef_im * b_re
    lb = lb_ref[...]
    e = jnp.exp(lb - jnp.max(lb, axis=0, keepdims=True))
    sm = e / jnp.sum(e, axis=0, keepdims=True)
    acc = jnp.zeros_like(sm[0:1])
    for l in range(lb.shape[0]):
        acc = acc + sm[l:l + 1]
        lbs_ref[l:l + 1, :] = acc - sm[0:1]


def _prep(a_re, a_im, log_dt, b_re, b_im, lower_bounds):
    L, G, P = a_re.shape
    H = b_re.shape[-1]
    are = a_re.reshape(L, G * P)
    aim = a_im.reshape(L, G * P)
    ldt = jnp.repeat(log_dt, P, axis=-1)
    arer = jnp.repeat(a_re.reshape(L * G, P), H, axis=-1)
    aimr = jnp.repeat(a_im.reshape(L * G, P), H, axis=-1)
    ldtr = jnp.broadcast_to(log_dt.reshape(L * G, 1), (L * G, P * H))
    out_shape = (
        jax.ShapeDtypeStruct((L, SUBLANES, G * P), f32),
        jax.ShapeDtypeStruct((L, SUBLANES, G * P), f32),
        jax.ShapeDtypeStruct((L * G, P * H), f32),
        jax.ShapeDtypeStruct((L * G, P * H), f32),
        jax.ShapeDtypeStruct(lower_bounds.shape, f32),
    )
    pwr, pwi, bbr, bbi, lbs = pl.pallas_call(_prep_kernel, out_shape=out_shape, name="param_prep")(
        are, aim, ldt, arer, aimr, ldtr, b_re.reshape(L * G, P * H), b_im.reshape(L * G, P * H), lower_bounds)
    return pwr, pwi, bbr.reshape(L, G, P, H), bbi.reshape(L, G, P, H), lbs


def _norm_matmul_kernel(x_ref, g_ref, w_ref, o_ref, h_ref):
    @pl.when(pl.program_id(1) == 0)
    def _():
        def body(r, carry):
            rows = pl.ds(pl.multiple_of(r * ROW_BLOCK, ROW_BLOCK), ROW_BLOCK)
            h_ref[rows, :] = _rms(x_ref[rows, :], g_ref[...]).astype(bf16)
            return carry
        lax.fori_loop(0, x_ref.shape[0] // ROW_BLOCK, body, 0)

    o_ref[...] = jnp.dot(h_ref[...], w_ref[...], preferred_element_type=f32)


def _norm_matmul(x, gain, w, l, tm, tn):
    M, D = x.shape
    N = w.shape[2]
    return pl.pallas_call(
        _norm_matmul_kernel,
        grid=(M // tm, N // tn),
        in_specs=[
            pl.BlockSpec((tm, D), lambda i, j: (i, 0)),
            pl.BlockSpec((1, D), lambda i, j: (0, 0)),
            pl.BlockSpec((None, D, tn), lambda i, j: (l, 0, j)),
        ],
        out_specs=pl.BlockSpec((tm, tn), lambda i, j: (i, j)),
        out_shape=jax.ShapeDtypeStruct((M, N), f32),
        scratch_shapes=[pltpu.VMEM((tm, D), bf16)],
        compiler_params=_cparams(("parallel", "arbitrary")),
        name="norm_in_proj",
    )(x, gain.reshape(1, D), w)


def _s5_scan_blocks(bu_ref, tab_ref, c0, blk_lo, blk_hi, W, lane_chunk):
    n_chunks = W // lane_chunk

    def block(r, carry):
        row0 = pl.multiple_of(r * SUBLANES, SUBLANES)
        new = []
        for c in range(n_chunks):
            lo = c * lane_chunk
            xr = bu_ref[pl.ds(row0, SUBLANES), lo:lo + lane_chunk]
            xi = bu_ref[pl.ds(row0, SUBLANES), W + lo:W + lo + lane_chunk]
            for n, d in enumerate((1, 2, 4)):
                tr = tab_ref[2 * n, :, lo:lo + lane_chunk]
                ti = tab_ref[2 * n + 1, :, lo:lo + lane_chunk]
                sr = pltpu.roll(xr, d, axis=0)
                si = pltpu.roll(xi, d, axis=0)
                xr, xi = xr + (tr * sr - ti * si), xi + (tr * si + ti * sr)
            pr = tab_ref[6, :, lo:lo + lane_chunk]
            pi_ = tab_ref[7, :, lo:lo + lane_chunk]
            cr = carry[2 * c]
            ci = carry[2 * c + 1]
            xr, xi = xr + (pr * cr - pi_ * ci), xi + (pr * ci + pi_ * cr)
            bu_ref[pl.ds(row0, SUBLANES), lo:lo + lane_chunk] = xr
            bu_ref[pl.ds(row0, SUBLANES), W + lo:W + lo + lane_chunk] = xi
            new.append(xr[SUBLANES - 1:SUBLANES])
            new.append(xi[SUBLANES - 1:SUBLANES])
        return tuple(new)

    init = []
    for c in range(n_chunks):
        lo = c * lane_chunk
        init.append(c0[:, lo:lo + lane_chunk])
        init.append(c0[:, W + lo:W + lo + lane_chunk])
    fin = lax.fori_loop(blk_lo, blk_hi, block, tuple(init))
    return jnp.concatenate([fin[2 * c] for c in range(n_chunks)] + [fin[2 * c + 1] for c in range(n_chunks)], axis=-1)


def _s5_kernel(u_ref, bexp_ref, cexp_ref, pwr_ref, pwi_ref, d_ref, h0p_ref, h0s_ref, wglu_ref, gain_ref,
               out_ref, hfp_ref, hfs_ref, bu_ref, tab_ref, z_ref,
               *, n_sets, lane_chunk, nt_prompt, n_seq_s, seq_s):
    i = pl.program_id(0)
    s = pl.program_id(1)
    tt = u_ref.shape[0]
    W = pwr_ref.shape[-1]

    @pl.when(i == 0)
    def _():
        hfp_ref[0, s] = h0p_ref[0, s]

    row = lax.broadcasted_iota(jnp.int32, (SUBLANES, W), 0)
    pwr = pwr_ref[...]
    pwi = pwi_ref[...]
    for n, d in enumerate((1, 2, 4)):
        keep = row >= d
        tab_ref[2 * n] = jnp.where(keep, jnp.broadcast_to(pwr[d - 1:d], (SUBLANES, W)), 0.0)
        tab_ref[2 * n + 1] = jnp.where(keep, jnp.broadcast_to(pwi[d - 1:d], (SUBLANES, W)), 0.0)
    tab_ref[6] = pwr
    tab_ref[7] = pwi

    def run(rows, segments):
        u = u_ref[0:rows, :]
        bu_ref[0:rows, :] = jnp.dot(u.astype(bf16), bexp_ref[...], preferred_element_type=f32)
        for blk_lo, blk_hi, src, dst, b in segments:
            dst[b, s] = _s5_scan_blocks(bu_ref, tab_ref, src[b, s], blk_lo, blk_hi, W, lane_chunk)
        y = jnp.dot(bu_ref[0:rows, :].astype(bf16), cexp_ref[...], preferred_element_type=f32) + d_ref[...] * u
        z_ref[s, 0:rows, :] = 0.5 * y * (1.0 + lax.erf(y * (1.0 / math.sqrt(2.0))))

        @pl.when(s == n_sets - 1)
        def _():
            z = jnp.concatenate([z_ref[k, 0:rows, :] for k in range(n_sets)], axis=-1)
            gl = jnp.dot(z.astype(bf16), wglu_ref[...], preferred_element_type=f32)
            out_ref[0:rows, :] = _rms(z * jax.nn.sigmoid(gl), gain_ref[...]).astype(out_ref.dtype)

    @pl.when(i < nt_prompt)
    def _():
        run(tt, [(0, tt // SUBLANES, hfp_ref, hfp_ref, 0)])

    @pl.when(i == nt_prompt)
    def _():
        bps = seq_s // SUBLANES
        run(n_seq_s * seq_s, [(b * bps, (b + 1) * bps, h0s_ref, hfs_ref, b) for b in range(n_seq_s)])


def _s5_group(proj, bexp, cexp, pwr, pwi, d, h0p, h0s, wglu, gain, l, n_prompt, seq_s):
    M = proj.shape[0]
    _, n_sets, cw, w2 = bexp.shape
    W = w2 // 2
    tt = SEQ_TILE
    nt_prompt = n_prompt // tt
    n_seq_s = h0s.shape[0]
    assert n_prompt % tt == 0 and n_seq_s * seq_s <= tt and M == n_prompt + n_seq_s * seq_s
    s5w = n_sets * cw
    kern = functools.partial(_s5_kernel, n_sets=n_sets, lane_chunk=min(256, W), nt_prompt=nt_prompt,
                             n_seq_s=n_seq_s, seq_s=seq_s)
    return pl.pallas_call(
        kern,
        grid=(nt_prompt + 1, n_sets),
        in_specs=[
            pl.BlockSpec((tt, cw), lambda i, s: (i, s)),
            pl.BlockSpec((None, None, cw, w2), lambda i, s: (l, s, 0, 0)),
            pl.BlockSpec((None, None, w2, cw), lambda i, s: (l, s, 0, 0)),
            pl.BlockSpec((None, None, SUBLANES, W), lambda i, s: (l, s, 0, 0)),
            pl.BlockSpec((None, None, SUBLANES, W), lambda i, s: (l, s, 0, 0)),
            pl.BlockSpec((None, None, 1, cw), lambda i, s: (l, s, 0, 0)),
            pl.BlockSpec((1, n_sets, 1, w2), lambda i, s: (0, 0, 0, 0)),
            pl.BlockSpec((n_seq_s, n_sets, 1, w2), lambda i, s: (0, 0, 0, 0)),
            pl.BlockSpec((None, s5w, s5w), lambda i, s: (l, 0, 0)),
            pl.BlockSpec((1, s5w), lambda i, s: (0, 0)),
        ],
        out_specs=[
            pl.BlockSpec((tt, s5w), lambda i, s: (i, 0)),
            pl.BlockSpec((1, n_sets, 1, w2), lambda i, s: (0, 0, 0, 0)),
            pl.BlockSpec((n_seq_s, n_sets, 1, w2), lambda i, s: (0, 0, 0, 0)),
        ],
        out_shape=[
            jax.ShapeDtypeStruct((M, s5w), bf16),
            jax.ShapeDtypeStruct((1, n_sets, 1, w2), f32),
            jax.ShapeDtypeStruct((n_seq_s, n_sets, 1, w2), f32),
        ],
        scratch_shapes=[
            pltpu.VMEM((tt, w2), f32),
            pltpu.VMEM((8, SUBLANES, W), f32),
            pltpu.VMEM((n_sets, tt, cw), f32),
        ],
        compiler_params=_cparams(("arbitrary", "arbitrary")),
        name="s5_group",
    )(proj, bexp, cexp, pwr, pwi, d, h0p, h0s, wglu, gain)


def _pair_levels(C):
    t = np.arange(C)[:, None]
    s = np.arange(C)[None, :]
    lvl = np.full((C, C), -1, np.int32)
    m, n = 1, 0
    while m < C:
        own = (t // (2 * m) == s // (2 * m)) & ((t // m) % 2 == 1) & ((s // m) % 2 == 0)
        lvl[own] = n
        m *= 2
        n += 1
    return lvl


def _cumsum_rows(x):
    C = x.shape[0]
    nb = C // SUBLANES
    x3 = x.reshape(nb, SUBLANES, x.shape[1])
    row = lax.broadcasted_iota(jnp.int32, (1, SUBLANES, x.shape[1]), 1)
    for d in (1, 2, 4):
        x3 = x3 + jnp.where(row >= d, pltpu.roll(x3, d, axis=1), 0.0)
    outs = [x3[0:1]]
    acc = x3[0:1, SUBLANES - 1:SUBLANES, :]
    for b in range(1, nb):
        outs.append(x3[b:b + 1] + acc)
        acc = acc + x3[b:b + 1, SUBLANES - 1:SUBLANES, :]
    return jnp.concatenate(outs, axis=0).reshape(C, x.shape[1])


def _dot_nt(a, b):
    return lax.dot_general(a, b, (((1,), (1,)), ((), ())), preferred_element_type=f32)


def _dot_tn(a, b):
    return lax.dot_general(a, b, (((0,), (0,)), ((), ())), preferred_element_type=f32)


def _hgrn_chunk(q, z, v, g, lb, gain, st, lvl):
    C, K = q.shape
    nb = C // SUBLANES
    qf = q * jax.nn.sigmoid(q)
    e = jnp.exp(-jnp.abs(z))
    r = 1.0 / (1.0 + e)
    er = e * r
    pos = z >= 0.0
    om = 1.0 - lb
    fc = jnp.maximum(lb + om * jnp.where(pos, r, er), F_TINY)
    logf = jnp.log(fc)
    kf = om * jnp.where(pos, er, r)
    bc = _cumsum_rows(logf)
    blast = bc[C - 1:C, :]
    vb = v.astype(bf16)

    o = _dot_nt((qf * jnp.exp(bc)).astype(bf16), st.astype(bf16))
    st_new = st * jnp.exp(blast) + _dot_tn(vb, (kf * jnp.exp(blast - bc)).astype(bf16))

    o = o + jnp.sum(qf * kf, axis=-1, keepdims=True) * v

    q3 = qf.reshape(nb, SUBLANES, K)
    k3 = kf.reshape(nb, SUBLANES, K)
    b3 = bc.reshape(nb, SUBLANES, K)
    sub = lax.broadcasted_iota(jnp.int32, (1, SUBLANES, K), 1)
    att = None
    m, n = 1, 0
    while m < C:
        if m == 1:
            upper = (sub & 1) == 1
            qm = jnp.where(upper, q3 * fc.reshape(nb, SUBLANES, K), 0.0).reshape(C, K)
            km = jnp.where(upper, 0.0, k3).reshape(C, K)
        elif m < SUBLANES:
            bref = jnp.broadcast_to(b3[:, m - 1:m, :], b3.shape)
            for base in range(2 * m, SUBLANES, 2 * m):
                bref = jnp.where(sub >= base, jnp.broadcast_to(b3[:, base + m - 1:base + m, :], b3.shape), bref)
            dec = jnp.exp(-jnp.abs(b3 - bref))
            upper = (sub & m) == m
            qm = jnp.where(upper, q3 * dec, 0.0).reshape(C, K)
            km = jnp.where(upper, 0.0, k3 * dec).reshape(C, K)
        else:
            nblk = C // (2 * m)
            b4 = bc.reshape(nblk, 2, m, K)
            dec = jnp.exp(-jnp.abs(b4 - b4[:, 0:1, m - 1:m, :]))
            zero = jnp.zeros((nblk, 1, m, K), f32)
            qm = jnp.concatenate([zero, qf.reshape(nblk, 2, m, K)[:, 1:2] * dec[:, 1:2]], axis=1).reshape(C, K)
            km = jnp.concatenate([kf.reshape(nblk, 2, m, K)[:, 0:1] * dec[:, 0:1], zero], axis=1).reshape(C, K)
        a = _dot_nt(qm.astype(bf16), km.astype(bf16))
        att = a if att is None else jnp.where(lvl == n, a, att)
        m *= 2
        n += 1
    att = jnp.where(lvl >= 0, att, 0.0)
    o = o + jnp.dot(att.astype(bf16), vb, preferred_element_type=f32)

    out = _rms(o, gain) * (g * jax.nn.sigmoid(g))
    return out, st_new


def _hgrn_kernel(q_ref, z_ref, v_ref, g_ref, lvlp_ref, lvls_ref, lb_ref, gain_ref, s0p_ref, s0s_ref,
                 out_ref, sfp_ref, sfs_ref, st_ref, *, heads, chunk_p, nt_prompt, n_seq_s, seq_s):
    i = pl.program_id(1)
    tt = q_ref.shape[0]
    K = LANES
    gain = gain_ref[...]

    def chunk(rows, h, st, lvl):
        ls = slice(h * K, (h + 1) * K)
        out, st_new = _hgrn_chunk(q_ref[rows, ls], z_ref[rows, ls], v_ref[rows, ls], g_ref[rows, ls],
                                  lb_ref[:, ls], gain, st, lvl)
        out_ref[rows, ls] = out.astype(out_ref.dtype)
        return st_new

    @pl.when(i == 0)
    def _():
        for h in range(heads):
            st_ref[h] = s0p_ref[0, h].T

    @pl.when(i < nt_prompt)
    def _():
        def body(c, carry):
            rows = pl.ds(pl.multiple_of(c * chunk_p, chunk_p), chunk_p)
            lvl = lvlp_ref[...]
            for h in range(heads):
                st_ref[h] = chunk(rows, h, st_ref[h], lvl)
            return carry

        lax.fori_loop(0, tt // chunk_p, body, 0, unroll=2)

    @pl.when(i == nt_prompt - 1)
    def _():
        for h in range(heads):
            sfp_ref[0, h] = st_ref[h].T

    @pl.when(i == nt_prompt)
    def _():
        lvl = lvls_ref[...]
        for b in range(n_seq_s):
            for h in range(heads):
                sfs_ref[b, h] = chunk(slice(b * seq_s, (b + 1) * seq_s), h, s0s_ref[b, h].T, lvl).T


def _hgrn_group(proj, lbs, gain, s0p, s0s, n_prompt, seq_s, col0):
    M = proj.shape[0]
    n_seq_s, H, K, V = s0s.shape
    assert K == LANES and V == LANES and s0p.shape[0] == 1
    heads = HG_HEADS_PER_STEP
    tt = SEQ_TILE
    nt_prompt = n_prompt // tt
    chunk_p = min(HG_CHUNK, n_prompt)
    assert seq_s <= HG_CHUNK and n_prompt % tt == 0 and tt % chunk_p == 0 and n_seq_s * seq_s <= tt
    assert M == n_prompt + n_seq_s * seq_s
    cb = col0 // (heads * K)
    hb = H // heads
    kern = functools.partial(_hgrn_kernel, heads=heads, chunk_p=chunk_p, nt_prompt=nt_prompt,
                             n_seq_s=n_seq_s, seq_s=seq_s)

    def col_spec(k):
        return pl.BlockSpec((tt, heads * K), lambda hp, i, k=k: (i, cb + k * hb + hp))

    return pl.pallas_call(
        kern,
        grid=(hb, nt_prompt + 1),
        in_specs=[
            col_spec(0), col_spec(1), col_spec(2), col_spec(3),
            pl.BlockSpec((chunk_p, chunk_p), lambda hp, i: (0, 0)),
            pl.BlockSpec((seq_s, seq_s), lambda hp, i: (0, 0)),
            pl.BlockSpec((1, heads * K), lambda hp, i: (0, hp)),
            pl.BlockSpec((1, V), lambda hp, i: (0, 0)),
            pl.BlockSpec((1, heads, K, V), lambda hp, i: (0, hp, 0, 0)),
            pl.BlockSpec((n_seq_s, heads, K, V), lambda hp, i: (0, hp, 0, 0)),
        ],
        out_specs=[
            pl.BlockSpec((tt, heads * V), lambda hp, i: (i, hp)),
            pl.BlockSpec((1, heads, K, V), lambda hp, i: (0, hp, 0, 0)),
            pl.BlockSpec((n_seq_s, heads, K, V), lambda hp, i: (0, hp, 0, 0)),
        ],
        out_shape=[
            jax.ShapeDtypeStruct((M, H * V), bf16),
            jax.ShapeDtypeStruct((1, H, K, V), f32),
            jax.ShapeDtypeStruct((n_seq_s, H, K, V), f32),
        ],
        scratch_shapes=[pltpu.VMEM((heads, V, K), f32)],
        compiler_params=_cparams(("arbitrary", "arbitrary")),
        name="hgrn2_group",
    )(proj, proj, proj, proj, jnp.asarray(_pair_levels(chunk_p)), jnp.asarray(_pair_levels(seq_s)),
      lbs, gain, s0p, s0s)


def _out_proj_kernel(x_ref, a_ref, b_ref, wa_ref, wb_ref, o_ref):
    acc = jnp.dot(a_ref[...], wa_ref[...], preferred_element_type=f32)
    acc = acc + jnp.dot(b_ref[...], wb_ref[...], preferred_element_type=f32)
    o_ref[...] = x_ref[...] + acc


def _out_proj(x, a, b, w, l, tm, tn):
    M, D = x.shape
    ka = a.shape[1]
    kb = b.shape[1]
    assert ka == kb
    return pl.pallas_call(
        _out_proj_kernel,
        grid=(M // tm, D // tn),
        in_specs=[
            pl.BlockSpec((tm, tn), lambda i, j: (i, j)),
            pl.BlockSpec((tm, ka), lambda i, j: (i, 0)),
            pl.BlockSpec((tm, kb), lambda i, j: (i, 0)),
            pl.BlockSpec((None, ka, tn), lambda i, j: (l, 0, j)),
            pl.BlockSpec((None, kb, tn), lambda i, j: (l, 1, j)),
        ],
        out_specs=pl.BlockSpec((tm, tn), lambda i, j: (i, j)),
        out_shape=jax.ShapeDtypeStruct((M, D), f32),
        compiler_params=_cparams(("parallel", "arbitrary")),
        name="out_proj",
    )(x, a, b, w, w)


def _ffn_kernel(x_ref, g_ref, wg_ref, wu_ref, wd_ref, gf_ref, o_ref, h_ref, *, final_norm):
    j = pl.program_id(1)

    tm, D = o_ref.shape

    def row_blocks(fn):
        def body(r, carry):
            fn(pl.ds(pl.multiple_of(r * ROW_BLOCK, ROW_BLOCK), ROW_BLOCK))
            return carry
        lax.fori_loop(0, tm // ROW_BLOCK, body, 0)

    @pl.when(j == 0)
    def _():
        def init(rows):
            x = x_ref[rows, :]
            h_ref[rows, :] = _rms(x, g_ref[...]).astype(bf16)
            o_ref[rows, :] = x
        row_blocks(init)

    h = h_ref[...]
    gate = jnp.dot(h, wg_ref[...], preferred_element_type=f32)
    up = jnp.dot(h, wu_ref[...], preferred_element_type=f32)
    act = (gate * jax.nn.sigmoid(gate) * up).astype(bf16)
    for n in range(0, D, COL_PANEL):
        o_ref[:, n:n + COL_PANEL] += jnp.dot(act, wd_ref[:, n:n + COL_PANEL], preferred_element_type=f32)

    if final_norm:
        @pl.when(j == pl.num_programs(1) - 1)
        def _():
            def close(rows):
                o_ref[rows, :] = _rms(o_ref[rows, :], gf_ref[...])
            row_blocks(close)


def _ffn(x, gain, wg, wu, wd, gain_final, final_norm, l, tm, tf):
    M, D = x.shape
    F = wg.shape[2]
    kern = functools.partial(_ffn_kernel, final_norm=final_norm)
    return pl.pallas_call(
        kern,
        grid=(M // tm, F // tf),
        in_specs=[
            pl.BlockSpec((tm, D), lambda i, j: (i, 0), pipeline_mode=pl.Buffered(1)),
            pl.BlockSpec((1, D), lambda i, j: (0, 0)),
            pl.BlockSpec((None, D, tf), lambda i, j: (l, 0, j)),
            pl.BlockSpec((None, D, tf), lambda i, j: (l, 0, j)),
            pl.BlockSpec((None, tf, D), lambda i, j: (l, j, 0)),
            pl.BlockSpec((1, D), lambda i, j: (0, 0)),
        ],
        out_specs=pl.BlockSpec((tm, D), lambda i, j: (i, 0)),
        out_shape=jax.ShapeDtypeStruct((M, D), f32),
        scratch_shapes=[pltpu.VMEM((tm, D), bf16)],
        compiler_params=_cparams(("parallel", "arbitrary")),
        name="swiglu_ffn",
    )(x, gain.reshape(1, D), wg, wu, wd, gain_final.reshape(1, D))


def _expand_s5_weights(bb_re, bb_im, c_re, c_im):
    L, G, P, H = bb_re.shape
    ns = G // S5_SET_GROUPS
    eye = jnp.eye(S5_SET_GROUPS, dtype=f32)

    def expand_b(bb):
        t = bb.reshape(L, ns, S5_SET_GROUPS, P, H)
        return jnp.einsum('lsgph,gk->lsghkp', t, eye).reshape(L, ns, S5_SET_GROUPS * H, S5_SET_GROUPS * P)

    def expand_c(c):
        t = c.reshape(L, ns, S5_SET_GROUPS, H, P)
        return jnp.einsum('lsghp,gk->lskpgh', t, eye).reshape(L, ns, S5_SET_GROUPS * P, S5_SET_GROUPS * H)

    bexp = jnp.concatenate([expand_b(bb_re), expand_b(bb_im)], axis=-1).astype(bf16)
    cexp = jnp.concatenate([expand_c(c_re), -expand_c(c_im)], axis=-2).astype(bf16)
    return bexp, cexp


def kernel(x_prompt, x_sample, state_s5_re, state_s5_im, state_hgrn, w_in, w_out, norm_mix, norm_ffn, norm_final, s5_a_re, s5_a_im, s5_log_dt, s5_b_re, s5_b_im, s5_c_re, s5_c_im, s5_d, s5_w_glu, s5_out_norm, hg_lower_bounds, hg_out_norm, ffn_w_gate, ffn_w_up, ffn_w_down):
    L, G, P = s5_a_re.shape
    Hs = s5_b_re.shape[-1]
    _, _, HH, HK, HV = state_hgrn.shape
    PB, PT, D = x_prompt.shape
    SB, ST, _ = x_sample.shape
    assert PB == 1
    ns = G // S5_SET_GROUPS
    W = S5_SET_GROUPS * P
    s5w = G * Hs
    n_prompt = PB * PT
    M = n_prompt + SB * ST
    assert M % ROW_TILE == 0

    pwr, pwi, bb_re, bb_im, lbs = _prep(s5_a_re, s5_a_im, s5_log_dt, s5_b_re, s5_b_im, hg_lower_bounds)
    bexp, cexp = _expand_s5_weights(bb_re, bb_im, s5_c_re, s5_c_im)
    pwr = pwr.reshape(L, SUBLANES, ns, W).transpose(0, 2, 1, 3)
    pwi = pwi.reshape(L, SUBLANES, ns, W).transpose(0, 2, 1, 3)
    d_sets = s5_d.reshape(L, ns, 1, S5_SET_GROUPS * Hs)

    w_in_b = w_in.astype(bf16)
    w_out_b = w_out.astype(bf16)
    w_glu_b = s5_w_glu.astype(bf16)
    wg_b = ffn_w_gate.astype(bf16)
    wu_b = ffn_w_up.astype(bf16)
    wd_b = ffn_w_down.astype(bf16)

    zero_s5 = jnp.zeros((PB, ns, 1, 2 * W), f32)
    zero_hg = jnp.zeros((PB, HH, HK, HV), f32)

    x = jnp.concatenate([x_prompt.reshape(n_prompt, D), x_sample.reshape(SB * ST, D)], axis=0)
    p_re, p_im, p_hg, s_re, s_im, s_hg = [], [], [], [], [], []
    for l in range(L):
        h0s = jnp.concatenate([state_s5_re[l].reshape(SB, ns, 1, W), state_s5_im[l].reshape(SB, ns, 1, W)], axis=-1)
        proj = _norm_matmul(x, norm_mix[l], w_in_b, l, ROW_TILE, 1024)
        s5_out, hfp, hfs = _s5_group(proj, bexp, cexp, pwr, pwi, d_sets, zero_s5, h0s, w_glu_b,
                                     s5_out_norm[l].reshape(1, s5w), l, n_prompt, ST)
        hg_out, sfp, sfs = _hgrn_group(proj, lbs[l].reshape(1, HH * HK), hg_out_norm[l].reshape(1, HV),
                                       zero_hg, state_hgrn[l], n_prompt, ST, s5w)
        x = _out_proj(x, s5_out, hg_out, w_out_b, l, ROW_TILE, 1024)
        x = _ffn(x, norm_ffn[l], wg_b, wu_b, wd_b, norm_final, l == L - 1, l, ROW_TILE, 256)
        p_re.append(hfp[..., :W].reshape(PB, G, P))
        p_im.append(hfp[..., W:].reshape(PB, G, P))
        p_hg.append(sfp)
        s_re.append(hfs[..., :W].reshape(SB, G, P))
        s_im.append(hfs[..., W:].reshape(SB, G, P))
        s_hg.append(sfs)
    y_prompt = x[:n_prompt].reshape(PB, PT, D)
    y_sample = x[n_prompt:].reshape(SB, ST, D)
    return (y_prompt, y_sample, jnp.stack(p_re), jnp.stack(p_im), jnp.stack(p_hg),
            jnp.stack(s_re), jnp.stack(s_im), jnp.stack(s_hg))
```

```python
import functools
import math

import jax
import jax.numpy as jnp
import numpy as np
from jax import lax
from jax.experimental import pallas as pl
from jax.experimental.pallas import tpu as pltpu

f32 = jnp.float32
bf16 = jnp.bfloat16

EPS = 1e-6
F_TINY = 1e-30
HG_CHUNK = 64
SUBLANES = 8
LANES = 128
S5_SET_GROUPS = 16
HG_HEADS_PER_STEP = 4
VMEM_LIMIT = 56 * 1024 * 1024
ROW_TILE = 640
SEQ_TILE = 512
ROW_BLOCK = 64
COL_PANEL = 512


def _cparams(sem):
    return pltpu.CompilerParams(dimension_semantics=sem, vmem_limit_bytes=VMEM_LIMIT)


def _rms(x, gain):
    return x * lax.rsqrt(jnp.mean(x * x, axis=-1, keepdims=True) + EPS) * gain


def _prep_kernel(are_ref, aim_ref, ldt_ref, arer_ref, aimr_ref, ldtr_ref, bre_ref, bim_ref, lb_ref,
                 p1r_ref, p1i_ref, pseg_r_ref, pseg_i_ref, bbr_ref, bbi_ref, lbs_ref):
    S = SEQ_TILE // SUBLANES
    GP = p1r_ref.shape[-1]
    dt = jnp.exp(ldt_ref[...])
    adt_re = are_ref[...] * dt
    adt_im = aim_ref[...] * dt
    mag = jnp.exp(adt_re)
    p1r_ref[...] = mag * jnp.cos(adt_im)
    p1i_ref[...] = mag * jnp.sin(adt_im)
    ks = ((lax.broadcasted_iota(jnp.int32, (SUBLANES, GP), 0) + 1) * S).astype(f32)
    mag = jnp.exp(adt_re * ks)
    pseg_r_ref[...] = mag * jnp.cos(adt_im * ks)
    pseg_i_ref[...] = mag * jnp.sin(adt_im * ks)
    a_re = arer_ref[...]
    a_im = aimr_ref[...]
    dt = jnp.exp(ldtr_ref[...])
    mag = jnp.exp(a_re * dt)
    lam_re = mag * jnp.cos(a_im * dt)
    lam_im = mag * jnp.sin(a_im * dt)
    den = a_re * a_re + a_im * a_im
    coef_re = ((lam_re - 1.0) * a_re + lam_im * a_im) / den
    coef_im = (lam_im * a_re - (lam_re - 1.0) * a_im) / den
    b_re = bre_ref[...]
    b_im = bim_ref[...]
    bbr_ref[...] = coef_re * b_re - coef_im * b_im
    bbi_ref[...] = coef_re * b_im + coef_im * b_re
    lb = lb_ref[...]
    e = jnp.exp(lb - jnp.max(lb, axis=0, keepdims=True))
    sm = e / jnp.sum(e, axis=0, keepdims=True)
    acc = jnp.zeros_like(sm[0:1])
    for l in range(lb.shape[0]):
        acc = acc + sm[l:l + 1]
        lbs_ref[l:l + 1, :] = acc - sm[0:1]


def _prep(a_re, a_im, log_dt, b_re, b_im, lower_bounds):
    L, G, P = a_re.shape
    H = b_re.shape[-1]
    GP = G * P
    PH = P * H
    are = a_re.reshape(L, 1, GP)
    aim = a_im.reshape(L, 1, GP)
    ldt = jnp.repeat(log_dt, P, axis=-1).reshape(L, 1, GP)
    arer = jnp.repeat(a_re.reshape(L * G, P), H, axis=-1)
    aimr = jnp.repeat(a_im.reshape(L * G, P), H, axis=-1)
    ldtr = jnp.broadcast_to(log_dt.reshape(L * G, 1), (L * G, PH))
    row_spec = pl.BlockSpec((None, 1, GP), lambda l: (l, 0, 0))
    grp_spec = pl.BlockSpec((G, PH), lambda l: (l, 0))
    lb_spec = pl.BlockSpec(lower_bounds.shape, lambda l: (0, 0))
    out_shape = (
        jax.ShapeDtypeStruct((L, 1, GP), f32),
        jax.ShapeDtypeStruct((L, 1, GP), f32),
        jax.ShapeDtypeStruct((L, SUBLANES, GP), f32),
        jax.ShapeDtypeStruct((L, SUBLANES, GP), f32),
        jax.ShapeDtypeStruct((L * G, PH), f32),
        jax.ShapeDtypeStruct((L * G, PH), f32),
        jax.ShapeDtypeStruct(lower_bounds.shape, f32),
    )
    p1r, p1i, psr, psi, bbr, bbi, lbs = pl.pallas_call(
        _prep_kernel,
        grid=(L,),
        in_specs=[row_spec, row_spec, row_spec, grp_spec, grp_spec, grp_spec, grp_spec, grp_spec, lb_spec],
        out_specs=[
            row_spec,
            row_spec,
            pl.BlockSpec((None, SUBLANES, GP), lambda l: (l, 0, 0)),
            pl.BlockSpec((None, SUBLANES, GP), lambda l: (l, 0, 0)),
            grp_spec, grp_spec, lb_spec,
        ],
        out_shape=out_shape,
        compiler_params=_cparams(("arbitrary",)),
        name="param_prep",
    )(are, aim, ldt, arer, aimr, ldtr, b_re.reshape(L * G, PH), b_im.reshape(L * G, PH), lower_bounds)
    return p1r, p1i, psr, psi, bbr.reshape(L, G, P, H), bbi.reshape(L, G, P, H), lbs


def _norm_matmul_kernel(x_ref, g_ref, w_ref, o_ref, h_ref):
    @pl.when(pl.program_id(1) == 0)
    def _():
        def body(r, carry):
            rows = pl.ds(pl.multiple_of(r * ROW_BLOCK, ROW_BLOCK), ROW_BLOCK)
            h_ref[rows, :] = _rms(x_ref[rows, :], g_ref[...]).astype(bf16)
            return carry
        lax.fori_loop(0, x_ref.shape[0] // ROW_BLOCK, body, 0)

    o_ref[...] = jnp.dot(h_ref[...], w_ref[...], preferred_element_type=f32)


def _norm_matmul(x, gain, w, l, tm, tn):
    M, D = x.shape
    N = w.shape[2]
    return pl.pallas_call(
        _norm_matmul_kernel,
        grid=(M // tm, N // tn),
        in_specs=[
            pl.BlockSpec((tm, D), lambda i, j: (i, 0)),
            pl.BlockSpec((1, D), lambda i, j: (0, 0)),
            pl.BlockSpec((None, D, tn), lambda i, j: (l, 0, j)),
        ],
        out_specs=pl.BlockSpec((tm, tn), lambda i, j: (i, j)),
        out_shape=jax.ShapeDtypeStruct((M, N), f32),
        scratch_shapes=[pltpu.VMEM((tm, D), bf16)],
        compiler_params=_cparams(("parallel", "arbitrary")),
        name="norm_in_proj",
    )(x, gain.reshape(1, D), w)


def _cmul(ar, ai, br, bi):
    return ar * br - ai * bi, ar * bi + ai * br


def _s5_kernel(ua_ref, ub_ref, bexp_ref, cexp_ref, p1r_ref, p1i_ref, psr_ref, psi_ref, d_ref,
               h0p_ref, h0s_ref, wglu_ref, gain_ref,
               out_ref, hfp_ref, hfs_ref, up_ref, bu_ref, hb_ref, lam_ref, z_ref,
               *, n_sets, lane_chunk, nt_prompt, n_seq_s, seq_s):
    i = pl.program_id(0)
    s = pl.program_id(1)
    tt, half = ua_ref.shape
    cw = 2 * half
    W = p1r_ref.shape[-1]
    LC = lane_chunk
    chunks = [c * LC for c in range(W // LC)]

    @pl.when(i == 0)
    def _():
        hfp_ref[0, s] = h0p_ref[0, s]

    lam_ref[0] = jnp.broadcast_to(p1r_ref[...], (SUBLANES, W))
    lam_ref[1] = jnp.broadcast_to(p1i_ref[...], (SUBLANES, W))

    def gather_rows(n_steps):
        def body(j, carry):
            dst = pl.ds(pl.multiple_of(j * SUBLANES, SUBLANES), SUBLANES)
            up_ref[dst, 0:half] = ua_ref[pl.ds(j, SUBLANES, stride=n_steps), :]
            up_ref[dst, half:cw] = ub_ref[pl.ds(j, SUBLANES, stride=n_steps), :]
            return carry
        lax.fori_loop(0, n_steps, body, 0, unroll=8)

    def in_proj(rows):
        ub = up_ref[0:rows, :].astype(bf16)
        bu_ref[0:rows, 0:W] = jnp.dot(ub, bexp_ref[:, 0:W], preferred_element_type=f32)
        bu_ref[0:rows, W:2 * W] = jnp.dot(ub, bexp_ref[:, W:2 * W], preferred_element_type=f32)

    def advance(x, rows):
        new = []
        for c, lo in enumerate(chunks):
            pr, pi_ = _cmul(lam_ref[0, :, lo:lo + LC], lam_ref[1, :, lo:lo + LC], x[2 * c], x[2 * c + 1])
            new += [pr + bu_ref[rows, lo:lo + LC], pi_ + bu_ref[rows, W + lo:W + lo + LC]]
        return new

    def scan(n_steps, x0, store):
        def two_steps(jj, x):
            r0 = pl.multiple_of(jj * 2 * SUBLANES, 2 * SUBLANES)
            xa = advance(x, pl.ds(r0, SUBLANES))
            xb = advance(xa, pl.ds(r0 + SUBLANES, SUBLANES))
            if store:
                rows = pl.ds(r0, 2 * SUBLANES)
                for c, lo in enumerate(chunks):
                    hb_ref[rows, lo:lo + LC] = jnp.concatenate([xa[2 * c], xb[2 * c]], axis=0).astype(bf16)
                    hb_ref[rows, W + lo:W + lo + LC] = jnp.concatenate([xa[2 * c + 1], xb[2 * c + 1]], axis=0).astype(bf16)
            return tuple(xb)
        return lax.fori_loop(0, n_steps // 2, two_steps, tuple(x0), unroll=2)

    def finish(rows, n_steps):
        y = (jnp.dot(hb_ref[0:rows, 0:W], cexp_ref[0:W, :], preferred_element_type=f32)
             + jnp.dot(hb_ref[0:rows, W:2 * W], cexp_ref[W:2 * W, :], preferred_element_type=f32)
             + d_ref[...] * up_ref[0:rows, :])
        up_ref[0:rows, :] = 0.5 * y * (1.0 + lax.erf(y * (1.0 / math.sqrt(2.0))))

        def scatter(j, carry):
            src = pl.ds(pl.multiple_of(j * SUBLANES, SUBLANES), SUBLANES)
            z_ref[2 * s, pl.ds(j, SUBLANES, stride=n_steps), :] = up_ref[src, 0:half]
            z_ref[2 * s + 1, pl.ds(j, SUBLANES, stride=n_steps), :] = up_ref[src, half:cw]
            return carry
        lax.fori_loop(0, n_steps, scatter, 0, unroll=8)

        @pl.when(s == n_sets - 1)
        def _():
            z = jnp.concatenate([z_ref[k, 0:rows, :] for k in range(2 * n_sets)], axis=-1)
            gl = jnp.dot(z.astype(bf16), wglu_ref[...], preferred_element_type=f32)
            out_ref[0:rows, :] = _rms(z * jax.nn.sigmoid(gl), gain_ref[...]).astype(out_ref.dtype)

    @pl.when(i < nt_prompt)
    def _():
        S = tt // SUBLANES
        gather_rows(S)
        in_proj(tt)
        zero = jnp.zeros((SUBLANES, LC), f32)
        fin = scan(S, [zero] * (2 * len(chunks)), store=False)

        row = lax.broadcasted_iota(jnp.int32, (SUBLANES, LC), 0)
        c_in = hfp_ref[0, s]
        entry = []
        c_out_r, c_out_i = [], []
        for c, lo in enumerate(chunks):
            qr = psr_ref[:, lo:lo + LC]
            qi = psi_ref[:, lo:lo + LC]
            er, ei = fin[2 * c], fin[2 * c + 1]
            for d in (1, 2, 4):
                keep = row >= d
                tr = jnp.where(keep, jnp.broadcast_to(qr[d - 1:d], (SUBLANES, LC)), 0.0)
                ti = jnp.where(keep, jnp.broadcast_to(qi[d - 1:d], (SUBLANES, LC)), 0.0)
                pr, pi_ = _cmul(tr, ti, pltpu.roll(er, d, axis=0), pltpu.roll(ei, d, axis=0))
                er, ei = er + pr, ei + pi_
            cr = c_in[:, lo:lo + LC]
            ci = c_in[:, W + lo:W + lo + LC]
            first = row == 0
            gr = jnp.where(first, 1.0, pltpu.roll(qr, 1, axis=0))
            gi = jnp.where(first, 0.0, pltpu.roll(qi, 1, axis=0))
            pr, pi_ = _cmul(gr, gi, cr, ci)
            entry.append(jnp.where(first, 0.0, pltpu.roll(er, 1, axis=0)) + pr)
            entry.append(jnp.where(first, 0.0, pltpu.roll(ei, 1, axis=0)) + pi_)
            pr, pi_ = _cmul(qr[SUBLANES - 1:SUBLANES], qi[SUBLANES - 1:SUBLANES], cr, ci)
            c_out_r.append(er[SUBLANES - 1:SUBLANES] + pr)
            c_out_i.append(ei[SUBLANES - 1:SUBLANES] + pi_)
        hfp_ref[0, s] = jnp.concatenate(c_out_r + c_out_i, axis=-1)

        scan(S, entry, store=True)
        finish(tt, S)

    @pl.when(i == nt_prompt)
    def _():
        rows = n_seq_s * seq_s
        gather_rows(seq_s)
        in_proj(rows)
        h0 = h0s_ref[s]
        x0 = []
        for lo in chunks:
            x0 += [h0[:, lo:lo + LC], h0[:, W + lo:W + lo + LC]]
        fin = scan(seq_s, x0, store=True)
        hfs_ref[s] = jnp.concatenate([fin[2 * c] for c in range(len(chunks))]
                                     + [fin[2 * c + 1] for c in range(len(chunks))], axis=-1)
        finish(rows, seq_s)


def _s5_group(proj, bexp, cexp, p1r, p1i, psr, psi, d, h0p, h0s, wglu, gain, l, n_prompt, seq_s):
    M = proj.shape[0]
    _, n_sets, cw, w2 = bexp.shape
    W = w2 // 2
    half = cw // 2
    tt = SEQ_TILE
    nt_prompt = n_prompt // tt
    n_seq_s = h0s.shape[1]
    assert half == LANES and n_seq_s == SUBLANES and seq_s % 4 == 0 and (tt // SUBLANES) % 4 == 0
    assert n_prompt % tt == 0 and n_seq_s * seq_s <= tt and M == n_prompt + n_seq_s * seq_s
    s5w = n_sets * cw
    kern = functools.partial(_s5_kernel, n_sets=n_sets, lane_chunk=min(256, W), nt_prompt=nt_prompt,
                             n_seq_s=n_seq_s, seq_s=seq_s)
    return pl.pallas_call(
        kern,
        grid=(nt_prompt + 1, n_sets),
        in_specs=[
            pl.BlockSpec((tt, half), lambda i, s: (i, 2 * s)),
            pl.BlockSpec((tt, half), lambda i, s: (i, 2 * s + 1)),
            pl.BlockSpec((None, None, cw, w2), lambda i, s: (l, s, 0, 0)),
            pl.BlockSpec((None, None, w2, cw), lambda i, s: (l, s, 0, 0)),
            pl.BlockSpec((None, 1, W), lambda i, s: (l, 0, s)),
            pl.BlockSpec((None, 1, W), lambda i, s: (l, 0, s)),
            pl.BlockSpec((None, SUBLANES, W), lambda i, s: (l, 0, s)),
            pl.BlockSpec((None, SUBLANES, W), lambda i, s: (l, 0, s)),
            pl.BlockSpec((None, None, 1, cw), lambda i, s: (l, s, 0, 0)),
            pl.BlockSpec((1, n_sets, 1, w2), lambda i, s: (0, 0, 0, 0)),
            pl.BlockSpec((n_sets, n_seq_s, w2), lambda i, s: (0, 0, 0)),
            pl.BlockSpec((None, s5w, s5w), lambda i, s: (l, 0, 0)),
            pl.BlockSpec((1, s5w), lambda i, s: (0, 0)),
        ],
        out_specs=[
            pl.BlockSpec((tt, s5w), lambda i, s: (i, 0)),
            pl.BlockSpec((1, n_sets, 1, w2), lambda i, s: (0, 0, 0, 0)),
            pl.BlockSpec((n_sets, n_seq_s, w2), lambda i, s: (0, 0, 0)),
        ],
        out_shape=[
            jax.ShapeDtypeStruct((M, s5w), bf16),
            jax.ShapeDtypeStruct((1, n_sets, 1, w2), f32),
            jax.ShapeDtypeStruct((n_sets, n_seq_s, w2), f32),
        ],
        scratch_shapes=[
            pltpu.VMEM((tt, cw), f32),
            pltpu.VMEM((tt, w2), f32),
            pltpu.VMEM((tt, w2), bf16),
            pltpu.VMEM((2, SUBLANES, W), f32),
            pltpu.VMEM((2 * n_sets, tt, half), f32),
        ],
        compiler_params=_cparams(("arbitrary", "arbitrary")),
        name="s5_group",
    )(proj, proj, bexp, cexp, p1r, p1i, psr, psi, d, h0p, h0s, wglu, gain)


def _pair_levels(C):
    t = np.arange(C)[:, None]
    s = np.arange(C)[None, :]
    lvl = np.full((C, C), -1, np.int32)
    m, n = 1, 0
    while m < C:
        own = (t // (2 * m) == s // (2 * m)) & ((t // m) % 2 == 1) & ((s // m) % 2 == 0)
        lvl[own] = n
        m *= 2
        n += 1
    return lvl


def _cumsum_rows(x):
    C = x.shape[0]
    nb = C // SUBLANES
    x3 = x.reshape(nb, SUBLANES, x.shape[1])
    row = lax.broadcasted_iota(jnp.int32, (1, SUBLANES, x.shape[1]), 1)
    for d in (1, 2, 4):
        x3 = x3 + jnp.where(row >= d, pltpu.roll(x3, d, axis=1), 0.0)
    outs = [x3[0:1]]
    acc = x3[0:1, SUBLANES - 1:SUBLANES, :]
    for b in range(1, nb):
        outs.append(x3[b:b + 1] + acc)
        acc = acc + x3[b:b + 1, SUBLANES - 1:SUBLANES, :]
    return jnp.concatenate(outs, axis=0).reshape(C, x.shape[1])


def _dot_nt(a, b):
    return lax.dot_general(a, b, (((1,), (1,)), ((), ())), preferred_element_type=f32)


def _dot_tn(a, b):
    return lax.dot_general(a, b, (((0,), (0,)), ((), ())), preferred_element_type=f32)


def _hgrn_chunk(q, z, v, g, lb, gain, st, lvl):
    C, K = q.shape
    nb = C // SUBLANES
    qf = q * jax.nn.sigmoid(q)
    e = jnp.exp(-jnp.abs(z))
    r = 1.0 / (1.0 + e)
    er = e * r
    pos = z >= 0.0
    om = 1.0 - lb
    fc = jnp.maximum(lb + om * jnp.where(pos, r, er), F_TINY)
    logf = jnp.log(fc)
    kf = om * jnp.where(pos, er, r)
    bc = _cumsum_rows(logf)
    blast = bc[C - 1:C, :]
    vb = v.astype(bf16)

    o = _dot_nt((qf * jnp.exp(bc)).astype(bf16), st.astype(bf16))
    st_new = st * jnp.exp(blast) + _dot_tn(vb, (kf * jnp.exp(blast - bc)).astype(bf16))

    o = o + jnp.sum(qf * kf, axis=-1, keepdims=True) * v

    q3 = qf.reshape(nb, SUBLANES, K)
    k3 = kf.reshape(nb, SUBLANES, K)
    b3 = bc.reshape(nb, SUBLANES, K)
    sub = lax.broadcasted_iota(jnp.int32, (1, SUBLANES, K), 1)
    att = None
    m, n = 1, 0
    while m < C:
        if m == 1:
            upper = (sub & 1) == 1
            qm = jnp.where(upper, q3 * fc.reshape(nb, SUBLANES, K), 0.0).reshape(C, K)
            km = jnp.where(upper, 0.0, k3).reshape(C, K)
        elif m < SUBLANES:
            bref = jnp.broadcast_to(b3[:, m - 1:m, :], b3.shape)
            for base in range(2 * m, SUBLANES, 2 * m):
                bref = jnp.where(sub >= base, jnp.broadcast_to(b3[:, base + m - 1:base + m, :], b3.shape), bref)
            dec = jnp.exp(-jnp.abs(b3 - bref))
            upper = (sub & m) == m
            qm = jnp.where(upper, q3 * dec, 0.0).reshape(C, K)
            km = jnp.where(upper, 0.0, k3 * dec).reshape(C, K)
        else:
            nblk = C // (2 * m)
            b4 = bc.reshape(nblk, 2, m, K)
            dec = jnp.exp(-jnp.abs(b4 - b4[:, 0:1, m - 1:m, :]))
            zero = jnp.zeros((nblk, 1, m, K), f32)
            qm = jnp.concatenate([zero, qf.reshape(nblk, 2, m, K)[:, 1:2] * dec[:, 1:2]], axis=1).reshape(C, K)
            km = jnp.concatenate([kf.reshape(nblk, 2, m, K)[:, 0:1] * dec[:, 0:1], zero], axis=1).reshape(C, K)
        a = _dot_nt(qm.astype(bf16), km.astype(bf16))
        att = a if att is None else jnp.where(lvl == n, a, att)
        m *= 2
        n += 1
    att = jnp.where(lvl >= 0, att, 0.0)
    o = o + jnp.dot(att.astype(bf16), vb, preferred_element_type=f32)

    out = _rms(o, gain) * (g * jax.nn.sigmoid(g))
    return out, st_new


def _hgrn_kernel(q_ref, z_ref, v_ref, g_ref, lvlp_ref, lvls_ref, lb_ref, gain_ref, s0p_ref, s0s_ref,
                 out_ref, sfp_ref, sfs_ref, st_ref, *, heads, chunk_p, nt_prompt, n_seq_s, seq_s):
    i = pl.program_id(1)
    tt = q_ref.shape[0]
    K = LANES
    gain = gain_ref[...]

    def chunk(rows, h, st, lvl):
        ls = slice(h * K, (h + 1) * K)
        out, st_new = _hgrn_chunk(q_ref[rows, ls], z_ref[rows, ls], v_ref[rows, ls], g_ref[rows, ls],
                                  lb_ref[:, ls], gain, st, lvl)
        out_ref[rows, ls] = out.astype(out_ref.dtype)
        return st_new

    @pl.when(i == 0)
    def _():
        for h in range(heads):
            st_ref[h] = s0p_ref[0, h].T

    @pl.when(i < nt_prompt)
    def _():
        def body(c, carry):
            rows = pl.ds(pl.multiple_of(c * chunk_p, chunk_p), chunk_p)
            lvl = lvlp_ref[...]
            for h in range(heads):
                st_ref[h] = chunk(rows, h, st_ref[h], lvl)
            return carry

        lax.fori_loop(0, tt // chunk_p, body, 0, unroll=2)

    @pl.when(i == nt_prompt - 1)
    def _():
        for h in range(heads):
            sfp_ref[0, h] = st_ref[h].T

    @pl.when(i == nt_prompt)
    def _():
        lvl = lvls_ref[...]
        for b in range(n_seq_s):
            for h in range(heads):
                sfs_ref[b, h] = chunk(slice(b * seq_s, (b + 1) * seq_s), h, s0s_ref[b, h].T, lvl).T


def _hgrn_group(proj, lbs, gain, s0p, s0s, n_prompt, seq_s, col0):
    M = proj.shape[0]
    n_seq_s, H, K, V = s0s.shape
    assert K == LANES and V == LANES and s0p.shape[0] == 1
    heads = HG_HEADS_PER_STEP
    tt = SEQ_TILE
    nt_prompt = n_prompt // tt
    chunk_p = min(HG_CHUNK, n_prompt)
    assert seq_s <= HG_CHUNK and n_prompt % tt == 0 and tt % chunk_p == 0 and n_seq_s * seq_s <= tt
    assert M == n_prompt + n_seq_s * seq_s
    cb = col0 // (heads * K)
    hb = H // heads
    kern = functools.partial(_hgrn_kernel, heads=heads, chunk_p=chunk_p, nt_prompt=nt_prompt,
                             n_seq_s=n_seq_s, seq_s=seq_s)

    def col_spec(k):
        return pl.BlockSpec((tt, heads * K), lambda hp, i, k=k: (i, cb + k * hb + hp))

    return pl.pallas_call(
        kern,
        grid=(hb, nt_prompt + 1),
        in_specs=[
            col_spec(0), col_spec(1), col_spec(2), col_spec(3),
            pl.BlockSpec((chunk_p, chunk_p), lambda hp, i: (0, 0)),
            pl.BlockSpec((seq_s, seq_s), lambda hp, i: (0, 0)),
            pl.BlockSpec((1, heads * K), lambda hp, i: (0, hp)),
            pl.BlockSpec((1, V), lambda hp, i: (0, 0)),
            pl.BlockSpec((1, heads, K, V), lambda hp, i: (0, hp, 0, 0)),
            pl.BlockSpec((n_seq_s, heads, K, V), lambda hp, i: (0, hp, 0, 0)),
        ],
        out_specs=[
            pl.BlockSpec((tt, heads * V), lambda hp, i: (i, hp)),
            pl.BlockSpec((1, heads, K, V), lambda hp, i: (0, hp, 0, 0)),
            pl.BlockSpec((n_seq_s, heads, K, V), lambda hp, i: (0, hp, 0, 0)),
        ],
        out_shape=[
            jax.ShapeDtypeStruct((M, H * V), bf16),
            jax.ShapeDtypeStruct((1, H, K, V), f32),
            jax.ShapeDtypeStruct((n_seq_s, H, K, V), f32),
        ],
        scratch_shapes=[pltpu.VMEM((heads, V, K), f32)],
        compiler_params=_cparams(("arbitrary", "arbitrary")),
        name="hgrn2_group",
    )(proj, proj, proj, proj, jnp.asarray(_pair_levels(chunk_p)), jnp.asarray(_pair_levels(seq_s)),
      lbs, gain, s0p, s0s)


def _out_proj_kernel(x_ref, a_ref, b_ref, wa_ref, wb_ref, o_ref):
    acc = jnp.dot(a_ref[...], wa_ref[...], preferred_element_type=f32)
    acc = acc + jnp.dot(b_ref[...], wb_ref[...], preferred_element_type=f32)
    o_ref[...] = x_ref[...] + acc


def _out_proj(x, a, b, w, l, tm, tn):
    M, D = x.shape
    ka = a.shape[1]
    kb = b.shape[1]
    assert ka == kb
    return pl.pallas_call(
        _out_proj_kernel,
        grid=(M // tm, D // tn),
        in_specs=[
            pl.BlockSpec((tm, tn), lambda i, j: (i, j)),
            pl.BlockSpec((tm, ka), lambda i, j: (i, 0)),
            pl.BlockSpec((tm, kb), lambda i, j: (i, 0)),
            pl.BlockSpec((None, ka, tn), lambda i, j: (l, 0, j)),
            pl.BlockSpec((None, kb, tn), lambda i, j: (l, 1, j)),
        ],
        out_specs=pl.BlockSpec((tm, tn), lambda i, j: (i, j)),
        out_shape=jax.ShapeDtypeStruct((M, D), f32),
        compiler_params=_cparams(("parallel", "arbitrary")),
        name="out_proj",
    )(x, a, b, w, w)


def _ffn_kernel(x_ref, g_ref, wg_ref, wu_ref, wd_ref, gf_ref, o_ref, h_ref, *, final_norm):
    j = pl.program_id(1)

    tm, D = o_ref.shape

    def row_blocks(fn):
        def body(r, carry):
            fn(pl.ds(pl.multiple_of(r * ROW_BLOCK, ROW_BLOCK), ROW_BLOCK))
            return carry
        lax.fori_loop(0, tm // ROW_BLOCK, body, 0)

    @pl.when(j == 0)
    def _():
        def init(rows):
            x = x_ref[rows, :]
            h_ref[rows, :] = _rms(x, g_ref[...]).astype(bf16)
            o_ref[rows, :] = x
        row_blocks(init)

    h = h_ref[...]
    gate = jnp.dot(h, wg_ref[...], preferred_element_type=f32)
    up = jnp.dot(h, wu_ref[...], preferred_element_type=f32)
    act = (gate * jax.nn.sigmoid(gate) * up).astype(bf16)
    for n in range(0, D, COL_PANEL):
        o_ref[:, n:n + COL_PANEL] += jnp.dot(act, wd_ref[:, n:n + COL_PANEL], preferred_element_type=f32)

    if final_norm:
        @pl.when(j == pl.num_programs(1) - 1)
        def _():
            def close(rows):
                o_ref[rows, :] = _rms(o_ref[rows, :], gf_ref[...])
            row_blocks(close)


def _ffn(x, gain, wg, wu, wd, gain_final, final_norm, l, tm, tf):
    M, D = x.shape
    F = wg.shape[2]
    kern = functools.partial(_ffn_kernel, final_norm=final_norm)
    return pl.pallas_call(
        kern,
        grid=(M // tm, F // tf),
        in_specs=[
            pl.BlockSpec((tm, D), lambda i, j: (i, 0), pipeline_mode=pl.Buffered(1)),
            pl.BlockSpec((1, D), lambda i, j: (0, 0)),
            pl.BlockSpec((None, D, tf), lambda i, j: (l, 0, j)),
            pl.BlockSpec((None, D, tf), lambda i, j: (l, 0, j)),
            pl.BlockSpec((None, tf, D), lambda i, j: (l, j, 0)),
            pl.BlockSpec((1, D), lambda i, j: (0, 0)),
        ],
        out_specs=pl.BlockSpec((tm, D), lambda i, j: (i, 0)),
        out_shape=jax.ShapeDtypeStruct((M, D), f32),
        scratch_shapes=[pltpu.VMEM((tm, D), bf16)],
        compiler_params=_cparams(("parallel", "arbitrary")),
        name="swiglu_ffn",
    )(x, gain.reshape(1, D), wg, wu, wd, gain_final.reshape(1, D))


def _expand_s5_weights(bb_re, bb_im, c_re, c_im):
    L, G, P, H = bb_re.shape
    ns = G // S5_SET_GROUPS
    grp_of_row = np.arange(S5_SET_GROUPS * H) // H
    grp_of_col = np.arange(S5_SET_GROUPS * P) // P
    same_group = jnp.asarray(grp_of_row[:, None] == grp_of_col[None, :], f32)

    def expand_b(bb):
        t = bb.reshape(L, ns, S5_SET_GROUPS, P, H).transpose(0, 1, 2, 4, 3).reshape(L, ns, S5_SET_GROUPS * H, P)
        return jnp.tile(t, (1, 1, 1, S5_SET_GROUPS)) * same_group

    def expand_c(c):
        t = c.reshape(L, ns, S5_SET_GROUPS * H, P).transpose(0, 1, 3, 2)
        return jnp.tile(t, (1, 1, S5_SET_GROUPS, 1)) * same_group.T

    bexp = jnp.concatenate([expand_b(bb_re), expand_b(bb_im)], axis=-1).astype(bf16)
    cexp = jnp.concatenate([expand_c(c_re), -expand_c(c_im)], axis=-2).astype(bf16)
    return bexp, cexp


def kernel(x_prompt, x_sample, state_s5_re, state_s5_im, state_hgrn, w_in, w_out, norm_mix, norm_ffn, norm_final, s5_a_re, s5_a_im, s5_log_dt, s5_b_re, s5_b_im, s5_c_re, s5_c_im, s5_d, s5_w_glu, s5_out_norm, hg_lower_bounds, hg_out_norm, ffn_w_gate, ffn_w_up, ffn_w_down):
    L, G, P = s5_a_re.shape
    Hs = s5_b_re.shape[-1]
    _, _, HH, HK, HV = state_hgrn.shape
    PB, PT, D = x_prompt.shape
    SB, ST, _ = x_sample.shape
    assert PB == 1
    ns = G // S5_SET_GROUPS
    W = S5_SET_GROUPS * P
    s5w = G * Hs
    n_prompt = PB * PT
    M = n_prompt + SB * ST
    assert M % ROW_TILE == 0

    p1r, p1i, psr, psi, bb_re, bb_im, lbs = _prep(s5_a_re, s5_a_im, s5_log_dt, s5_b_re, s5_b_im, hg_lower_bounds)
    bexp, cexp = _expand_s5_weights(bb_re, bb_im, s5_c_re, s5_c_im)
    d_sets = s5_d.reshape(L, ns, 1, S5_SET_GROUPS * Hs)

    w_in_b = w_in.astype(bf16)
    w_out_b = w_out.astype(bf16)
    w_glu_b = s5_w_glu.astype(bf16)
    wg_b = ffn_w_gate.astype(bf16)
    wu_b = ffn_w_up.astype(bf16)
    wd_b = ffn_w_down.astype(bf16)

    zero_s5 = jnp.zeros((PB, ns, 1, 2 * W), f32)
    zero_hg = jnp.zeros((PB, HH, HK, HV), f32)

    x = jnp.concatenate([x_prompt.reshape(n_prompt, D), x_sample.reshape(SB * ST, D)], axis=0)
    p_re, p_im, p_hg, s_re, s_im, s_hg = [], [], [], [], [], []
    for l in range(L):
        h0s = jnp.concatenate([state_s5_re[l].reshape(SB, ns, W), state_s5_im[l].reshape(SB, ns, W)],
                              axis=-1).transpose(1, 0, 2)
        proj = _norm_matmul(x, norm_mix[l], w_in_b, l, ROW_TILE, 1024)
        s5_out, hfp, hfs = _s5_group(proj, bexp, cexp, p1r, p1i, psr, psi, d_sets, zero_s5, h0s, w_glu_b,
                                     s5_out_norm[l].reshape(1, s5w), l, n_prompt, ST)
        hfs = hfs.transpose(1, 0, 2)
        hg_out, sfp, sfs = _hgrn_group(proj, lbs[l].reshape(1, HH * HK), hg_out_norm[l].reshape(1, HV),
                                       zero_hg, state_hgrn[l], n_prompt, ST, s5w)
        x = _out_proj(x, s5_out, hg_out, w_out_b, l, ROW_TILE, 1024)
        x = _ffn(x, norm_ffn[l], wg_b, wu_b, wd_b, norm_final, l == L - 1, l, ROW_TILE, 256)
        p_re.append(hfp[..., :W].reshape(PB, G, P))
        p_im.append(hfp[..., W:].reshape(PB, G, P))
        p_hg.append(sfp)
        s_re.append(hfs[..., :W].reshape(SB, G, P))
        s_im.append(hfs[..., W:].reshape(SB, G, P))
        s_hg.append(sfs)
    y_prompt = x[:n_prompt].reshape(PB, PT, D)
    y_sample = x[n_prompt:].reshape(SB, ST, D)
    return (y_prompt, y_sample, jnp.stack(p_re), jnp.stack(p_im), jnp.stack(p_hg),
            jnp.stack(s_re), jnp.stack(s_im), jnp.stack(s_hg))
```

```python
import functools
import math

import jax
import jax.numpy as jnp
import numpy as np
from jax import lax
from jax.experimental import pallas as pl
from jax.experimental.pallas import tpu as pltpu

f32 = jnp.float32
bf16 = jnp.bfloat16

EPS = 1e-6
F_TINY = 1e-30
HG_CHUNK = 64
SUBLANES = 8
LANES = 128
S5_SET_GROUPS = 16
HG_HEADS_PER_STEP = 4
VMEM_LIMIT = 56 * 1024 * 1024
ROW_TILE = 640
SEQ_TILE = 512
ROW_BLOCK = 64
COL_PANEL = 512
CAST_BLOCK = 256


def _cparams(sem):
    return pltpu.CompilerParams(dimension_semantics=sem, vmem_limit_bytes=VMEM_LIMIT)


def _rms(x, gain):
    return x * lax.rsqrt(jnp.mean(x * x, axis=-1, keepdims=True) + EPS) * gain


def _prep_kernel(are_ref, aim_ref, ldt_ref, arer_ref, aimr_ref, ldtr_ref, bre_ref, bim_ref, lb_ref,
                 p1r_ref, p1i_ref, pseg_r_ref, pseg_i_ref, bbr_ref, bbi_ref, lbs_ref):
    S = SEQ_TILE // SUBLANES
    GP = p1r_ref.shape[-1]
    dt = jnp.exp(ldt_ref[...])
    adt_re = are_ref[...] * dt
    adt_im = aim_ref[...] * dt
    mag = jnp.exp(adt_re)
    p1r_ref[...] = mag * jnp.cos(adt_im)
    p1i_ref[...] = mag * jnp.sin(adt_im)
    ks = ((lax.broadcasted_iota(jnp.int32, (SUBLANES, GP), 0) + 1) * S).astype(f32)
    mag = jnp.exp(adt_re * ks)
    pseg_r_ref[...] = mag * jnp.cos(adt_im * ks)
    pseg_i_ref[...] = mag * jnp.sin(adt_im * ks)
    a_re = arer_ref[...]
    a_im = aimr_ref[...]
    dt = jnp.exp(ldtr_ref[...])
    mag = jnp.exp(a_re * dt)
    lam_re = mag * jnp.cos(a_im * dt)
    lam_im = mag * jnp.sin(a_im * dt)
    den = a_re * a_re + a_im * a_im
    coef_re = ((lam_re - 1.0) * a_re + lam_im * a_im) / den
    coef_im = (lam_im * a_re - (lam_re - 1.0) * a_im) / den
    b_re = bre_ref[...]
    b_im = bim_ref[...]
    bbr_ref[...] = coef_re * b_re - coef_im * b_im
    bbi_ref[...] = coef_re * b_im + coef_im * b_re
    lb = lb_ref[...]
    e = jnp.exp(lb - jnp.max(lb, axis=0, keepdims=True))
    sm = e / jnp.sum(e, axis=0, keepdims=True)
    acc = jnp.zeros_like(sm[0:1])
    for l in range(lb.shape[0]):
        acc = acc + sm[l:l + 1]
        lbs_ref[l:l + 1, :] = acc - sm[0:1]


def _prep(a_re, a_im, log_dt, b_re, b_im, lower_bounds):
    L, G, P = a_re.shape
    H = b_re.shape[-1]
    GP = G * P
    PH = P * H
    are = a_re.reshape(L, 1, GP)
    aim = a_im.reshape(L, 1, GP)
    ldt = jnp.repeat(log_dt, P, axis=-1).reshape(L, 1, GP)
    arer = jnp.repeat(a_re.reshape(L * G, P), H, axis=-1)
    aimr = jnp.repeat(a_im.reshape(L * G, P), H, axis=-1)
    ldtr = jnp.broadcast_to(log_dt.reshape(L * G, 1), (L * G, PH))
    row_spec = pl.BlockSpec((None, 1, GP), lambda l: (l, 0, 0))
    grp_spec = pl.BlockSpec((G, PH), lambda l: (l, 0))
    lb_spec = pl.BlockSpec(lower_bounds.shape, lambda l: (0, 0))
    out_shape = (
        jax.ShapeDtypeStruct((L, 1, GP), f32),
        jax.ShapeDtypeStruct((L, 1, GP), f32),
        jax.ShapeDtypeStruct((L, SUBLANES, GP), f32),
        jax.ShapeDtypeStruct((L, SUBLANES, GP), f32),
        jax.ShapeDtypeStruct((L * G, PH), f32),
        jax.ShapeDtypeStruct((L * G, PH), f32),
        jax.ShapeDtypeStruct(lower_bounds.shape, f32),
    )
    p1r, p1i, psr, psi, bbr, bbi, lbs = pl.pallas_call(
        _prep_kernel,
        grid=(L,),
        in_specs=[row_spec, row_spec, row_spec, grp_spec, grp_spec, grp_spec, grp_spec, grp_spec, lb_spec],
        out_specs=[
            row_spec,
            row_spec,
            pl.BlockSpec((None, SUBLANES, GP), lambda l: (l, 0, 0)),
            pl.BlockSpec((None, SUBLANES, GP), lambda l: (l, 0, 0)),
            grp_spec, grp_spec, lb_spec,
        ],
        out_shape=out_shape,
        compiler_params=_cparams(("arbitrary",)),
        name="param_prep",
    )(are, aim, ldt, arer, aimr, ldtr, b_re.reshape(L * G, PH), b_im.reshape(L * G, PH), lower_bounds)
    return p1r, p1i, psr, psi, bbr.reshape(L, G, P, H), bbi.reshape(L, G, P, H), lbs


def _norm_matmul_kernel(x_ref, g_ref, w_ref, o_ref, h_ref):
    @pl.when(pl.program_id(1) == 0)
    def _():
        def body(r, carry):
            rows = pl.ds(pl.multiple_of(r * ROW_BLOCK, ROW_BLOCK), ROW_BLOCK)
            h_ref[rows, :] = _rms(x_ref[rows, :], g_ref[...]).astype(bf16)
            return carry
        lax.fori_loop(0, x_ref.shape[0] // ROW_BLOCK, body, 0)

    o_ref[...] = jnp.dot(h_ref[...], w_ref[...], preferred_element_type=f32)


def _norm_matmul(x, gain, w, l, tm, tn):
    M, D = x.shape
    N = w.shape[2]
    return pl.pallas_call(
        _norm_matmul_kernel,
        grid=(M // tm, N // tn),
        in_specs=[
            pl.BlockSpec((tm, D), lambda i, j: (i, 0)),
            pl.BlockSpec((1, D), lambda i, j: (0, 0)),
            pl.BlockSpec((None, D, tn), lambda i, j: (l, 0, j)),
        ],
        out_specs=pl.BlockSpec((tm, tn), lambda i, j: (i, j)),
        out_shape=jax.ShapeDtypeStruct((M, N), f32),
        scratch_shapes=[pltpu.VMEM((tm, D), bf16)],
        compiler_params=_cparams(("parallel", "arbitrary")),
        name="norm_in_proj",
    )(x, gain.reshape(1, D), w)


def _cmul(ar, ai, br, bi):
    return ar * br - ai * bi, ar * bi + ai * br


def _s5_kernel(ua_ref, ub_ref, bexp_ref, cexp_ref, p1r_ref, p1i_ref, psr_ref, psi_ref, d_ref,
               h0p_ref, h0s_ref, wglu_ref, gain_ref, *rest,
               n_sets, lane_chunk, nt_prompt, n_seq_s, seq_s, n_cast):
    cast_src = rest[:n_cast]
    out_ref, hfp_ref, hfs_ref = rest[n_cast:n_cast + 3]
    cast_dst = rest[n_cast + 3:2 * n_cast + 3]
    up_ref, bu_ref, hb_ref, lam_ref, z_ref = rest[2 * n_cast + 3:]
    cast_refs = list(zip(cast_src, cast_dst))
    _s5_body(ua_ref, ub_ref, bexp_ref, cexp_ref, p1r_ref, p1i_ref, psr_ref, psi_ref, d_ref,
             h0p_ref, h0s_ref, wglu_ref, gain_ref, cast_refs,
             out_ref, hfp_ref, hfs_ref, up_ref, bu_ref, hb_ref, lam_ref, z_ref,
             n_sets=n_sets, lane_chunk=lane_chunk, nt_prompt=nt_prompt, n_seq_s=n_seq_s, seq_s=seq_s)


def _s5_body(ua_ref, ub_ref, bexp_ref, cexp_ref, p1r_ref, p1i_ref, psr_ref, psi_ref, d_ref,
             h0p_ref, h0s_ref, wglu_ref, gain_ref, cast_refs,
             out_ref, hfp_ref, hfs_ref, up_ref, bu_ref, hb_ref, lam_ref, z_ref,
             *, n_sets, lane_chunk, nt_prompt, n_seq_s, seq_s):
    i = pl.program_id(0)
    s = pl.program_id(1)
    tt, half = ua_ref.shape
    cw = 2 * half
    W = p1r_ref.shape[-1]
    LC = lane_chunk
    chunks = [c * LC for c in range(W // LC)]

    @pl.when(i == 0)
    def _():
        hfp_ref[0, s] = h0p_ref[0, s]

    lam_ref[0] = jnp.broadcast_to(p1r_ref[...], (SUBLANES, W))
    lam_ref[1] = jnp.broadcast_to(p1i_ref[...], (SUBLANES, W))

    def gather_rows(n_steps):
        def body(j, carry):
            dst = pl.ds(pl.multiple_of(j * SUBLANES, SUBLANES), SUBLANES)
            up_ref[dst, 0:half] = ua_ref[pl.ds(j, SUBLANES, stride=n_steps), :]
            up_ref[dst, half:cw] = ub_ref[pl.ds(j, SUBLANES, stride=n_steps), :]
            return carry
        lax.fori_loop(0, n_steps, body, 0, unroll=8)

    def in_proj(rows):
        bu_ref[0:rows, :] = jnp.dot(up_ref[0:rows, :].astype(bf16), bexp_ref[...], preferred_element_type=f32)

    def advance(x, rows):
        new = []
        for c, lo in enumerate(chunks):
            pr, pi_ = _cmul(lam_ref[0, :, lo:lo + LC], lam_ref[1, :, lo:lo + LC], x[2 * c], x[2 * c + 1])
            new += [pr + bu_ref[rows, lo:lo + LC], pi_ + bu_ref[rows, W + lo:W + lo + LC]]
        return new

    def scan(n_steps, x0, store):
        def two_steps(jj, x):
            r0 = pl.multiple_of(jj * 2 * SUBLANES, 2 * SUBLANES)
            xa = advance(x, pl.ds(r0, SUBLANES))
            xb = advance(xa, pl.ds(r0 + SUBLANES, SUBLANES))
            if store:
                rows = pl.ds(r0, 2 * SUBLANES)
                for c, lo in enumerate(chunks):
                    hb_ref[rows, lo:lo + LC] = jnp.concatenate([xa[2 * c], xb[2 * c]], axis=0).astype(bf16)
                    hb_ref[rows, W + lo:W + lo + LC] = jnp.concatenate([xa[2 * c + 1], xb[2 * c + 1]], axis=0).astype(bf16)
            return tuple(xb)
        return lax.fori_loop(0, n_steps // 2, two_steps, tuple(x0), unroll=2)

    def finish(rows, n_steps):
        y = jnp.dot(hb_ref[0:rows, :], cexp_ref[...], preferred_element_type=f32) + d_ref[...] * up_ref[0:rows, :]
        up_ref[0:rows, :] = 0.5 * y * (1.0 + lax.erf(y * (1.0 / math.sqrt(2.0))))
        for src, dst in cast_refs:
            dst[...] = src[...].astype(bf16)

        def scatter(j, carry):
            src = pl.ds(pl.multiple_of(j * SUBLANES, SUBLANES), SUBLANES)
            z_ref[2 * s, pl.ds(j, SUBLANES, stride=n_steps), :] = up_ref[src, 0:half]
            z_ref[2 * s + 1, pl.ds(j, SUBLANES, stride=n_steps), :] = up_ref[src, half:cw]
            return carry
        lax.fori_loop(0, n_steps, scatter, 0, unroll=8)

        @pl.when(s == n_sets - 1)
        def _():
            z = jnp.concatenate([z_ref[k, 0:rows, :] for k in range(2 * n_sets)], axis=-1)
            gl = jnp.dot(z.astype(bf16), wglu_ref[...], preferred_element_type=f32)
            out_ref[0:rows, :] = _rms(z * jax.nn.sigmoid(gl), gain_ref[...]).astype(out_ref.dtype)

    @pl.when(i < nt_prompt)
    def _():
        S = tt // SUBLANES
        gather_rows(S)
        in_proj(tt)
        zero = jnp.zeros((SUBLANES, LC), f32)
        fin = scan(S, [zero] * (2 * len(chunks)), store=False)

        row = lax.broadcasted_iota(jnp.int32, (SUBLANES, LC), 0)
        c_in = hfp_ref[0, s]
        entry = []
        c_out_r, c_out_i = [], []
        for c, lo in enumerate(chunks):
            qr = psr_ref[:, lo:lo + LC]
            qi = psi_ref[:, lo:lo + LC]
            er, ei = fin[2 * c], fin[2 * c + 1]
            for d in (1, 2, 4):
                keep = row >= d
                tr = jnp.where(keep, jnp.broadcast_to(qr[d - 1:d], (SUBLANES, LC)), 0.0)
                ti = jnp.where(keep, jnp.broadcast_to(qi[d - 1:d], (SUBLANES, LC)), 0.0)
                pr, pi_ = _cmul(tr, ti, pltpu.roll(er, d, axis=0), pltpu.roll(ei, d, axis=0))
                er, ei = er + pr, ei + pi_
            cr = c_in[:, lo:lo + LC]
            ci = c_in[:, W + lo:W + lo + LC]
            first = row == 0
            gr = jnp.where(first, 1.0, pltpu.roll(qr, 1, axis=0))
            gi = jnp.where(first, 0.0, pltpu.roll(qi, 1, axis=0))
            pr, pi_ = _cmul(gr, gi, cr, ci)
            entry.append(jnp.where(first, 0.0, pltpu.roll(er, 1, axis=0)) + pr)
            entry.append(jnp.where(first, 0.0, pltpu.roll(ei, 1, axis=0)) + pi_)
            pr, pi_ = _cmul(qr[SUBLANES - 1:SUBLANES], qi[SUBLANES - 1:SUBLANES], cr, ci)
            c_out_r.append(er[SUBLANES - 1:SUBLANES] + pr)
            c_out_i.append(ei[SUBLANES - 1:SUBLANES] + pi_)
        hfp_ref[0, s] = jnp.concatenate(c_out_r + c_out_i, axis=-1)

        scan(S, entry, store=True)
        finish(tt, S)

    @pl.when(i == nt_prompt)
    def _():
        rows = n_seq_s * seq_s
        gather_rows(seq_s)
        in_proj(rows)
        h0 = h0s_ref[s]
        x0 = []
        for lo in chunks:
            x0 += [h0[:, lo:lo + LC], h0[:, W + lo:W + lo + LC]]
        fin = scan(seq_s, x0, store=True)
        hfs_ref[s] = jnp.concatenate([fin[2 * c] for c in range(len(chunks))]
                                     + [fin[2 * c + 1] for c in range(len(chunks))], axis=-1)
        finish(rows, seq_s)


def _s5_group(proj, bexp, cexp, p1r, p1i, psr, psi, d, h0p, h0s, wglu, gain, l, n_prompt, seq_s, casts=()):
    M = proj.shape[0]
    _, n_sets, cw, w2 = bexp.shape
    W = w2 // 2
    half = cw // 2
    tt = SEQ_TILE
    nt_prompt = n_prompt // tt
    n_seq_s = h0s.shape[1]
    assert half == LANES and n_seq_s == SUBLANES and seq_s % 4 == 0 and (tt // SUBLANES) % 4 == 0
    assert n_prompt % tt == 0 and n_seq_s * seq_s <= tt and M == n_prompt + n_seq_s * seq_s
    s5w = n_sets * cw
    kern = functools.partial(_s5_kernel, n_sets=n_sets, lane_chunk=min(256, W), nt_prompt=nt_prompt,
                             n_seq_s=n_seq_s, seq_s=seq_s, n_cast=len(casts))

    cast_in, cast_out, cast_shapes = [], [], []
    for w, wl in casts:
        _, rows_w, cols_w = w.shape
        n_panels = cols_w // LANES
        assert cols_w % LANES == 0 and (nt_prompt + 1) * n_sets >= n_panels

        def panel(i, s, n_panels=n_panels):
            return jnp.minimum(i * n_sets + s, n_panels - 1)

        cast_in.append(pl.BlockSpec((None, rows_w, LANES), lambda i, s, wl=wl, panel=panel: (wl, 0, panel(i, s))))
        cast_out.append(pl.BlockSpec((None, rows_w, LANES), lambda i, s, panel=panel: (0, 0, panel(i, s))))
        cast_shapes.append(jax.ShapeDtypeStruct((1, rows_w, cols_w), bf16))

    return pl.pallas_call(
        kern,
        grid=(nt_prompt + 1, n_sets),
        in_specs=[
            pl.BlockSpec((tt, half), lambda i, s: (i, 2 * s)),
            pl.BlockSpec((tt, half), lambda i, s: (i, 2 * s + 1)),
            pl.BlockSpec((None, None, cw, w2), lambda i, s: (l, s, 0, 0)),
            pl.BlockSpec((None, None, w2, cw), lambda i, s: (l, s, 0, 0)),
            pl.BlockSpec((None, 1, W), lambda i, s: (l, 0, s)),
            pl.BlockSpec((None, 1, W), lambda i, s: (l, 0, s)),
            pl.BlockSpec((None, SUBLANES, W), lambda i, s: (l, 0, s)),
            pl.BlockSpec((None, SUBLANES, W), lambda i, s: (l, 0, s)),
            pl.BlockSpec((None, None, 1, cw), lambda i, s: (l, s, 0, 0)),
            pl.BlockSpec((1, n_sets, 1, w2), lambda i, s: (0, 0, 0, 0)),
            pl.BlockSpec((n_sets, n_seq_s, w2), lambda i, s: (0, 0, 0)),
            pl.BlockSpec((None, s5w, s5w), lambda i, s: (l, 0, 0), pipeline_mode=pl.Buffered(1)),
            pl.BlockSpec((1, s5w), lambda i, s: (0, 0)),
        ] + cast_in,
        out_specs=[
            pl.BlockSpec((tt, s5w), lambda i, s: (i, 0)),
            pl.BlockSpec((1, n_sets, 1, w2), lambda i, s: (0, 0, 0, 0)),
            pl.BlockSpec((n_sets, n_seq_s, w2), lambda i, s: (0, 0, 0)),
        ] + cast_out,
        out_shape=[
            jax.ShapeDtypeStruct((M, s5w), bf16),
            jax.ShapeDtypeStruct((1, n_sets, 1, w2), f32),
            jax.ShapeDtypeStruct((n_sets, n_seq_s, w2), f32),
        ] + cast_shapes,
        scratch_shapes=[
            pltpu.VMEM((tt, cw), f32),
            pltpu.VMEM((tt, w2), f32),
            pltpu.VMEM((tt, w2), bf16),
            pltpu.VMEM((2, SUBLANES, W), f32),
            pltpu.VMEM((2 * n_sets, tt, half), f32),
        ],
        compiler_params=_cparams(("arbitrary", "arbitrary")),
        name="s5_group",
    )(proj, proj, bexp, cexp, p1r, p1i, psr, psi, d, h0p, h0s, wglu, gain, *[w for w, _ in casts])


def _pair_levels(C):
    t = np.arange(C)[:, None]
    s = np.arange(C)[None, :]
    lvl = np.full((C, C), -1, np.int32)
    m, n = 1, 0
    while m < C:
        own = (t // (2 * m) == s // (2 * m)) & ((t // m) % 2 == 1) & ((s // m) % 2 == 0)
        lvl[own] = n
        m *= 2
        n += 1
    return lvl


def _cumsum_rows(x):
    C = x.shape[0]
    nb = C // SUBLANES
    x3 = x.reshape(nb, SUBLANES, x.shape[1])
    row = lax.broadcasted_iota(jnp.int32, (1, SUBLANES, x.shape[1]), 1)
    for d in (1, 2, 4):
        x3 = x3 + jnp.where(row >= d, pltpu.roll(x3, d, axis=1), 0.0)
    outs = [x3[0:1]]
    acc = x3[0:1, SUBLANES - 1:SUBLANES, :]
    for b in range(1, nb):
        outs.append(x3[b:b + 1] + acc)
        acc = acc + x3[b:b + 1, SUBLANES - 1:SUBLANES, :]
    return jnp.concatenate(outs, axis=0).reshape(C, x.shape[1])


def _dot_nt(a, b):
    return lax.dot_general(a, b, (((1,), (1,)), ((), ())), preferred_element_type=f32)


def _dot_tn(a, b):
    return lax.dot_general(a, b, (((0,), (0,)), ((), ())), preferred_element_type=f32)


def _hgrn_chunk(q, z, v, g, lb, gain, st, lvl):
    C, K = q.shape
    nb = C // SUBLANES
    qf = q * jax.nn.sigmoid(q)
    e = jnp.exp(-jnp.abs(z))
    r = 1.0 / (1.0 + e)
    er = e * r
    pos = z >= 0.0
    om = 1.0 - lb
    fc = jnp.maximum(lb + om * jnp.where(pos, r, er), F_TINY)
    logf = jnp.log(fc)
    kf = om * jnp.where(pos, er, r)
    bc = _cumsum_rows(logf)
    blast = bc[C - 1:C, :]
    vb = v.astype(bf16)

    o = _dot_nt((qf * jnp.exp(bc)).astype(bf16), st.astype(bf16))
    st_new = st * jnp.exp(blast) + _dot_tn(vb, (kf * jnp.exp(blast - bc)).astype(bf16))

    o = o + jnp.sum(qf * kf, axis=-1, keepdims=True) * v

    q3 = qf.reshape(nb, SUBLANES, K)
    k3 = kf.reshape(nb, SUBLANES, K)
    b3 = bc.reshape(nb, SUBLANES, K)
    sub = lax.broadcasted_iota(jnp.int32, (1, SUBLANES, K), 1)
    att = None
    m, n = 1, 0
    while m < C:
        if m == 1:
            upper = (sub & 1) == 1
            qm = jnp.where(upper, q3 * fc.reshape(nb, SUBLANES, K), 0.0).reshape(C, K)
            km = jnp.where(upper, 0.0, k3).reshape(C, K)
        elif m < SUBLANES:
            bref = jnp.broadcast_to(b3[:, m - 1:m, :], b3.shape)
            for base in range(2 * m, SUBLANES, 2 * m):
                bref = jnp.where(sub >= base, jnp.broadcast_to(b3[:, base + m - 1:base + m, :], b3.shape), bref)
            dec = jnp.exp(-jnp.abs(b3 - bref))
            upper = (sub & m) == m
            qm = jnp.where(upper, q3 * dec, 0.0).reshape(C, K)
            km = jnp.where(upper, 0.0, k3 * dec).reshape(C, K)
        else:
            nblk = C // (2 * m)
            b4 = bc.reshape(nblk, 2, m, K)
            dec = jnp.exp(-jnp.abs(b4 - b4[:, 0:1, m - 1:m, :]))
            zero = jnp.zeros((nblk, 1, m, K), f32)
            qm = jnp.concatenate([zero, qf.reshape(nblk, 2, m, K)[:, 1:2] * dec[:, 1:2]], axis=1).reshape(C, K)
            km = jnp.concatenate([kf.reshape(nblk, 2, m, K)[:, 0:1] * dec[:, 0:1], zero], axis=1).reshape(C, K)
        a = _dot_nt(qm.astype(bf16), km.astype(bf16))
        att = a if att is None else jnp.where(lvl == n, a, att)
        m *= 2
        n += 1
    att = jnp.where(lvl >= 0, att, 0.0)
    o = o + jnp.dot(att.astype(bf16), vb, preferred_element_type=f32)

    out = _rms(o, gain) * (g * jax.nn.sigmoid(g))
    return out, st_new


def _hgrn_kernel(q_ref, z_ref, v_ref, g_ref, lvlp_ref, lvls_ref, lb_ref, gain_ref, s0p_ref, s0s_ref,
                 wg_ref, wu_ref, wd_ref,
                 out_ref, sfp_ref, sfs_ref, wgb_ref, wub_ref, wdb_ref, st_ref,
                 *, heads, chunk_p, nt_prompt, n_seq_s, seq_s):
    i = pl.program_id(1)
    tt = q_ref.shape[0]
    K = LANES
    gain = gain_ref[...]
    n_cast = tt // chunk_p

    def cast_part(c):
        def part(n):
            return slice(c * n, (c + 1) * n) if isinstance(c, int) else pl.ds(pl.multiple_of(c * n, n), n)
        rows = part(wg_ref.shape[0] // n_cast)
        wgb_ref[rows, :] = wg_ref[rows, :].astype(bf16)
        wub_ref[rows, :] = wu_ref[rows, :].astype(bf16)
        rows = part(wd_ref.shape[0] // n_cast)
        wdb_ref[rows, :] = wd_ref[rows, :].astype(bf16)

    def chunk(rows, h, st, lvl):
        ls = slice(h * K, (h + 1) * K)
        out, st_new = _hgrn_chunk(q_ref[rows, ls], z_ref[rows, ls], v_ref[rows, ls], g_ref[rows, ls],
                                  lb_ref[:, ls], gain, st, lvl)
        out_ref[rows, ls] = out.astype(out_ref.dtype)
        return st_new

    @pl.when(i == 0)
    def _():
        for h in range(heads):
            st_ref[h] = s0p_ref[0, h].T

    @pl.when(i < nt_prompt)
    def _():
        def body(c, carry):
            rows = pl.ds(pl.multiple_of(c * chunk_p, chunk_p), chunk_p)
            lvl = lvlp_ref[...]
            for h in range(heads):
                st_ref[h] = chunk(rows, h, st_ref[h], lvl)
            cast_part(c)
            return carry

        lax.fori_loop(0, tt // chunk_p, body, 0, unroll=2)

    @pl.when(i == nt_prompt - 1)
    def _():
        for h in range(heads):
            sfp_ref[0, h] = st_ref[h].T

    @pl.when(i == nt_prompt)
    def _():
        lvl = lvls_ref[...]
        for b in range(n_seq_s):
            for h in range(heads):
                sfs_ref[b, h] = chunk(slice(b * seq_s, (b + 1) * seq_s), h, s0s_ref[b, h].T, lvl).T
        for c in range(n_cast):
            cast_part(c)


def _hgrn_group(proj, lbs, gain, s0p, s0s, wg, wu, wd, l, n_prompt, seq_s, col0):
    M = proj.shape[0]
    _, D, F = wg.shape
    n_panels = F // CAST_BLOCK
    n_seq_s, H, K, V = s0s.shape
    assert K == LANES and V == LANES and s0p.shape[0] == 1
    heads = HG_HEADS_PER_STEP
    tt = SEQ_TILE
    nt_prompt = n_prompt // tt
    chunk_p = min(HG_CHUNK, n_prompt)
    assert seq_s <= HG_CHUNK and n_prompt % tt == 0 and tt % chunk_p == 0 and n_seq_s * seq_s <= tt
    assert M == n_prompt + n_seq_s * seq_s
    cb = col0 // (heads * K)
    hb = H // heads
    kern = functools.partial(_hgrn_kernel, heads=heads, chunk_p=chunk_p, nt_prompt=nt_prompt,
                             n_seq_s=n_seq_s, seq_s=seq_s)

    def col_spec(k):
        return pl.BlockSpec((tt, heads * K), lambda hp, i, k=k: (i, cb + k * hb + hp))

    n_steps = nt_prompt + 1
    assert hb * n_steps >= n_panels and F % CAST_BLOCK == 0

    def panel(hp, i):
        return jnp.minimum(hp * n_steps + i, n_panels - 1)

    return pl.pallas_call(
        kern,
        grid=(hb, n_steps),
        in_specs=[
            col_spec(0), col_spec(1), col_spec(2), col_spec(3),
            pl.BlockSpec((chunk_p, chunk_p), lambda hp, i: (0, 0)),
            pl.BlockSpec((seq_s, seq_s), lambda hp, i: (0, 0)),
            pl.BlockSpec((1, heads * K), lambda hp, i: (0, hp)),
            pl.BlockSpec((1, V), lambda hp, i: (0, 0)),
            pl.BlockSpec((1, heads, K, V), lambda hp, i: (0, hp, 0, 0)),
            pl.BlockSpec((n_seq_s, heads, K, V), lambda hp, i: (0, hp, 0, 0), pipeline_mode=pl.Buffered(1)),
            pl.BlockSpec((None, D, CAST_BLOCK), lambda hp, i: (l, 0, panel(hp, i))),
            pl.BlockSpec((None, D, CAST_BLOCK), lambda hp, i: (l, 0, panel(hp, i))),
            pl.BlockSpec((None, CAST_BLOCK, D), lambda hp, i: (l, panel(hp, i), 0)),
        ],
        out_specs=[
            pl.BlockSpec((tt, heads * V), lambda hp, i: (i, hp)),
            pl.BlockSpec((1, heads, K, V), lambda hp, i: (0, hp, 0, 0)),
            pl.BlockSpec((n_seq_s, heads, K, V), lambda hp, i: (0, hp, 0, 0)),
            pl.BlockSpec((None, D, CAST_BLOCK), lambda hp, i: (0, 0, panel(hp, i))),
            pl.BlockSpec((None, D, CAST_BLOCK), lambda hp, i: (0, 0, panel(hp, i))),
            pl.BlockSpec((None, CAST_BLOCK, D), lambda hp, i: (0, panel(hp, i), 0)),
        ],
        out_shape=[
            jax.ShapeDtypeStruct((M, H * V), bf16),
            jax.ShapeDtypeStruct((1, H, K, V), f32),
            jax.ShapeDtypeStruct((n_seq_s, H, K, V), f32),
            jax.ShapeDtypeStruct((1, D, F), bf16),
            jax.ShapeDtypeStruct((1, D, F), bf16),
            jax.ShapeDtypeStruct((1, F, D), bf16),
        ],
        scratch_shapes=[pltpu.VMEM((heads, V, K), f32)],
        compiler_params=_cparams(("arbitrary", "arbitrary")),
        name="hgrn2_group",
    )(proj, proj, proj, proj, jnp.asarray(_pair_levels(chunk_p)), jnp.asarray(_pair_levels(seq_s)),
      lbs, gain, s0p, s0s, wg, wu, wd)


def _out_proj_kernel(x_ref, a_ref, b_ref, wa_ref, wb_ref, o_ref):
    acc = jnp.dot(a_ref[...], wa_ref[...], preferred_element_type=f32)
    acc = acc + jnp.dot(b_ref[...], wb_ref[...], preferred_element_type=f32)
    o_ref[...] = x_ref[...] + acc


def _out_proj(x, a, b, w, l, tm, tn):
    M, D = x.shape
    ka = a.shape[1]
    kb = b.shape[1]
    assert ka == kb
    return pl.pallas_call(
        _out_proj_kernel,
        grid=(M // tm, D // tn),
        in_specs=[
            pl.BlockSpec((tm, tn), lambda i, j: (i, j)),
            pl.BlockSpec((tm, ka), lambda i, j: (i, 0)),
            pl.BlockSpec((tm, kb), lambda i, j: (i, 0)),
            pl.BlockSpec((None, ka, tn), lambda i, j: (l, 0, j)),
            pl.BlockSpec((None, kb, tn), lambda i, j: (l, 1, j)),
        ],
        out_specs=pl.BlockSpec((tm, tn), lambda i, j: (i, j)),
        out_shape=jax.ShapeDtypeStruct((M, D), f32),
        compiler_params=_cparams(("parallel", "arbitrary")),
        name="out_proj",
    )(x, a, b, w, w)


def _ffn_kernel(x_ref, g_ref, wg_ref, wu_ref, wd_ref, gf_ref, o_ref, h_ref, *, final_norm):
    j = pl.program_id(1)

    tm, D = o_ref.shape

    def row_blocks(fn):
        def body(r, carry):
            fn(pl.ds(pl.multiple_of(r * ROW_BLOCK, ROW_BLOCK), ROW_BLOCK))
            return carry
        lax.fori_loop(0, tm // ROW_BLOCK, body, 0)

    @pl.when(j == 0)
    def _():
        def init(rows):
            x = x_ref[rows, :]
            h_ref[rows, :] = _rms(x, g_ref[...]).astype(bf16)
            o_ref[rows, :] = x
        row_blocks(init)

    h = h_ref[...]
    gate = jnp.dot(h, wg_ref[...], preferred_element_type=f32)
    up = jnp.dot(h, wu_ref[...], preferred_element_type=f32)
    act = (gate * jax.nn.sigmoid(gate) * up).astype(bf16)
    for n in range(0, D, COL_PANEL):
        o_ref[:, n:n + COL_PANEL] += jnp.dot(act, wd_ref[:, n:n + COL_PANEL], preferred_element_type=f32)

    if final_norm:
        @pl.when(j == pl.num_programs(1) - 1)
        def _():
            def close(rows):
                o_ref[rows, :] = _rms(o_ref[rows, :], gf_ref[...])
            row_blocks(close)


def _ffn(x, gain, wg, wu, wd, gain_final, final_norm, l, tm, tf):
    M, D = x.shape
    F = wg.shape[2]
    kern = functools.partial(_ffn_kernel, final_norm=final_norm)
    return pl.pallas_call(
        kern,
        grid=(M // tm, F // tf),
        in_specs=[
            pl.BlockSpec((tm, D), lambda i, j: (i, 0), pipeline_mode=pl.Buffered(1)),
            pl.BlockSpec((1, D), lambda i, j: (0, 0)),
            pl.BlockSpec((None, D, tf), lambda i, j: (l, 0, j)),
            pl.BlockSpec((None, D, tf), lambda i, j: (l, 0, j)),
            pl.BlockSpec((None, tf, D), lambda i, j: (l, j, 0)),
            pl.BlockSpec((1, D), lambda i, j: (0, 0)),
        ],
        out_specs=pl.BlockSpec((tm, D), lambda i, j: (i, 0)),
        out_shape=jax.ShapeDtypeStruct((M, D), f32),
        scratch_shapes=[pltpu.VMEM((tm, D), bf16)],
        compiler_params=_cparams(("parallel", "arbitrary")),
        name="swiglu_ffn",
    )(x, gain.reshape(1, D), wg, wu, wd, gain_final.reshape(1, D))


def _expand_s5_weights(bb_re, bb_im, c_re, c_im):
    L, G, P, H = bb_re.shape
    ns = G // S5_SET_GROUPS
    grp_of_row = np.arange(S5_SET_GROUPS * H) // H
    grp_of_col = np.arange(S5_SET_GROUPS * P) // P
    same_group = jnp.asarray(grp_of_row[:, None] == grp_of_col[None, :], f32)

    def expand_b(bb):
        t = bb.reshape(L, ns, S5_SET_GROUPS, P, H).transpose(0, 1, 2, 4, 3).reshape(L, ns, S5_SET_GROUPS * H, P)
        return jnp.tile(t, (1, 1, 1, S5_SET_GROUPS)) * same_group

    def expand_c(c):
        t = c.reshape(L, ns, S5_SET_GROUPS * H, P).transpose(0, 1, 3, 2)
        return jnp.tile(t, (1, 1, S5_SET_GROUPS, 1)) * same_group.T

    bexp = jnp.concatenate([expand_b(bb_re), expand_b(bb_im)], axis=-1).astype(bf16)
    cexp = jnp.concatenate([expand_c(c_re), -expand_c(c_im)], axis=-2).astype(bf16)
    return bexp, cexp


def kernel(x_prompt, x_sample, state_s5_re, state_s5_im, state_hgrn, w_in, w_out, norm_mix, norm_ffn, norm_final, s5_a_re, s5_a_im, s5_log_dt, s5_b_re, s5_b_im, s5_c_re, s5_c_im, s5_d, s5_w_glu, s5_out_norm, hg_lower_bounds, hg_out_norm, ffn_w_gate, ffn_w_up, ffn_w_down):
    L, G, P = s5_a_re.shape
    Hs = s5_b_re.shape[-1]
    _, _, HH, HK, HV = state_hgrn.shape
    PB, PT, D = x_prompt.shape
    SB, ST, _ = x_sample.shape
    assert PB == 1
    ns = G // S5_SET_GROUPS
    W = S5_SET_GROUPS * P
    s5w = G * Hs
    n_prompt = PB * PT
    M = n_prompt + SB * ST
    assert M % ROW_TILE == 0

    p1r, p1i, psr, psi, bb_re, bb_im, lbs = _prep(s5_a_re, s5_a_im, s5_log_dt, s5_b_re, s5_b_im, hg_lower_bounds)
    bexp, cexp = _expand_s5_weights(bb_re, bb_im, s5_c_re, s5_c_im)
    d_sets = s5_d.reshape(L, ns, 1, S5_SET_GROUPS * Hs)

    w_in_b = w_in[0:1].astype(bf16)
    w_glu_b = s5_w_glu.astype(bf16)

    zero_s5 = jnp.zeros((PB, ns, 1, 2 * W), f32)
    zero_hg = jnp.zeros((PB, HH, HK, HV), f32)

    x = jnp.concatenate([x_prompt.reshape(n_prompt, D), x_sample.reshape(SB * ST, D)], axis=0)
    p_re, p_im, p_hg, s_re, s_im, s_hg = [], [], [], [], [], []
    for l in range(L):
        h0s = jnp.concatenate([state_s5_re[l].reshape(SB, ns, W), state_s5_im[l].reshape(SB, ns, W)],
                              axis=-1).transpose(1, 0, 2)
        proj = _norm_matmul(x, norm_mix[l], w_in_b, 0, ROW_TILE, 1024)
        casts = [(w_out, l)] + ([(w_in, l + 1)] if l + 1 < L else [])
        s5_out, hfp, hfs, w_out_b, *w_in_next = _s5_group(
            proj, bexp, cexp, p1r, p1i, psr, psi, d_sets, zero_s5, h0s, w_glu_b,
            s5_out_norm[l].reshape(1, s5w), l, n_prompt, ST, casts)
        if w_in_next:
            w_in_b = w_in_next[0]
        hfs = hfs.transpose(1, 0, 2)
        hg_out, sfp, sfs, wg_b, wu_b, wd_b = _hgrn_group(
            proj, lbs[l].reshape(1, HH * HK), hg_out_norm[l].reshape(1, HV), zero_hg, state_hgrn[l],
            ffn_w_gate, ffn_w_up, ffn_w_down, l, n_prompt, ST, s5w)
        x = _out_proj(x, s5_out, hg_out, w_out_b, 0, ROW_TILE, 1024)
        x = _ffn(x, norm_ffn[l], wg_b, wu_b, wd_b, norm_final, l == L - 1, 0, ROW_TILE, 256)
        p_re.append(hfp[..., :W].reshape(PB, G, P))
        p_im.append(hfp[..., W:].reshape(PB, G, P))
        p_hg.append(sfp)
        s_re.append(hfs[..., :W].reshape(SB, G, P))
        s_im.append(hfs[..., W:].reshape(SB, G, P))
        s_hg.append(sfs)
    y_prompt = x[:n_prompt].reshape(PB, PT, D)
    y_sample = x[n_prompt:].reshape(SB, ST, D)
    return (y_prompt, y_sample, jnp.stack(p_re), jnp.stack(p_im), jnp.stack(p_hg),
            jnp.stack(s_re), jnp.stack(s_im), jnp.stack(s_hg))
```

```python
import functools
import math

import jax
import jax.numpy as jnp
import numpy as np
from jax import lax
from jax.experimental import pallas as pl
from jax.experimental.pallas import tpu as pltpu

f32 = jnp.float32
bf16 = jnp.bfloat16

EPS = 1e-6
F_TINY = 1e-30
HG_CHUNK = 64
SUBLANES = 8
LANES = 128
S5_SET_GROUPS = 16
HG_HEADS_PER_STEP = 4
VMEM_LIMIT = 56 * 1024 * 1024
ROW_TILE = 640
SEQ_TILE = 512
ROW_BLOCK = 64
COL_PANEL = 512
CAST_BLOCK = 256


def _cparams(sem):
    return pltpu.CompilerParams(dimension_semantics=sem, vmem_limit_bytes=VMEM_LIMIT)


def _rms(x, gain):
    return x * lax.rsqrt(jnp.mean(x * x, axis=-1, keepdims=True) + EPS) * gain


def _prep_kernel(are_ref, aim_ref, ldt_ref, arer_ref, aimr_ref, ldtr_ref, bre_ref, bim_ref, lb_ref,
                 p1r_ref, p1i_ref, pseg_r_ref, pseg_i_ref, bbr_ref, bbi_ref, lbs_ref):
    S = SEQ_TILE // SUBLANES
    GP = p1r_ref.shape[-1]
    dt = jnp.exp(ldt_ref[...])
    adt_re = are_ref[...] * dt
    adt_im = aim_ref[...] * dt
    mag = jnp.exp(adt_re)
    p1r_ref[...] = mag * jnp.cos(adt_im)
    p1i_ref[...] = mag * jnp.sin(adt_im)
    ks = ((lax.broadcasted_iota(jnp.int32, (SUBLANES, GP), 0) + 1) * S).astype(f32)
    mag = jnp.exp(adt_re * ks)
    pseg_r_ref[...] = mag * jnp.cos(adt_im * ks)
    pseg_i_ref[...] = mag * jnp.sin(adt_im * ks)
    a_re = arer_ref[...]
    a_im = aimr_ref[...]
    dt = jnp.exp(ldtr_ref[...])
    mag = jnp.exp(a_re * dt)
    lam_re = mag * jnp.cos(a_im * dt)
    lam_im = mag * jnp.sin(a_im * dt)
    den = a_re * a_re + a_im * a_im
    coef_re = ((lam_re - 1.0) * a_re + lam_im * a_im) / den
    coef_im = (lam_im * a_re - (lam_re - 1.0) * a_im) / den
    b_re = bre_ref[...]
    b_im = bim_ref[...]
    bbr_ref[...] = coef_re * b_re - coef_im * b_im
    bbi_ref[...] = coef_re * b_im + coef_im * b_re
    lb = lb_ref[...]
    e = jnp.exp(lb - jnp.max(lb, axis=0, keepdims=True))
    sm = e / jnp.sum(e, axis=0, keepdims=True)
    acc = jnp.zeros_like(sm[0:1])
    for l in range(lb.shape[0]):
        acc = acc + sm[l:l + 1]
        lbs_ref[l:l + 1, :] = acc - sm[0:1]


def _prep(a_re, a_im, log_dt, b_re, b_im, lower_bounds):
    L, G, P = a_re.shape
    H = b_re.shape[-1]
    GP = G * P
    PH = P * H
    are = a_re.reshape(L, 1, GP)
    aim = a_im.reshape(L, 1, GP)
    ldt = jnp.repeat(log_dt, P, axis=-1).reshape(L, 1, GP)
    arer = jnp.repeat(a_re.reshape(L * G, P), H, axis=-1)
    aimr = jnp.repeat(a_im.reshape(L * G, P), H, axis=-1)
    ldtr = jnp.broadcast_to(log_dt.reshape(L * G, 1), (L * G, PH))
    row_spec = pl.BlockSpec((None, 1, GP), lambda l: (l, 0, 0))
    grp_spec = pl.BlockSpec((G, PH), lambda l: (l, 0))
    lb_spec = pl.BlockSpec(lower_bounds.shape, lambda l: (0, 0))
    out_shape = (
        jax.ShapeDtypeStruct((L, 1, GP), f32),
        jax.ShapeDtypeStruct((L, 1, GP), f32),
        jax.ShapeDtypeStruct((L, SUBLANES, GP), f32),
        jax.ShapeDtypeStruct((L, SUBLANES, GP), f32),
        jax.ShapeDtypeStruct((L * G, PH), f32),
        jax.ShapeDtypeStruct((L * G, PH), f32),
        jax.ShapeDtypeStruct(lower_bounds.shape, f32),
    )
    p1r, p1i, psr, psi, bbr, bbi, lbs = pl.pallas_call(
        _prep_kernel,
        grid=(L,),
        in_specs=[row_spec, row_spec, row_spec, grp_spec, grp_spec, grp_spec, grp_spec, grp_spec, lb_spec],
        out_specs=[
            row_spec,
            row_spec,
            pl.BlockSpec((None, SUBLANES, GP), lambda l: (l, 0, 0)),
            pl.BlockSpec((None, SUBLANES, GP), lambda l: (l, 0, 0)),
            grp_spec, grp_spec, lb_spec,
        ],
        out_shape=out_shape,
        compiler_params=_cparams(("arbitrary",)),
        name="param_prep",
    )(are, aim, ldt, arer, aimr, ldtr, b_re.reshape(L * G, PH), b_im.reshape(L * G, PH), lower_bounds)
    return p1r, p1i, psr, psi, bbr.reshape(L, G, P, H), bbi.reshape(L, G, P, H), lbs


def _norm_matmul_kernel(x_ref, g_ref, w_ref, o_ref, h_ref):
    @pl.when(pl.program_id(1) == 0)
    def _():
        def body(r, carry):
            rows = pl.ds(pl.multiple_of(r * ROW_BLOCK, ROW_BLOCK), ROW_BLOCK)
            h_ref[rows, :] = _rms(x_ref[rows, :], g_ref[...]).astype(bf16)
            return carry
        lax.fori_loop(0, x_ref.shape[0] // ROW_BLOCK, body, 0)

    o_ref[...] = jnp.dot(h_ref[...], w_ref[...], preferred_element_type=f32)


def _norm_matmul(x, gain, w, l, tm, tn):
    M, D = x.shape
    N = w.shape[2]
    return pl.pallas_call(
        _norm_matmul_kernel,
        grid=(M // tm, N // tn),
        in_specs=[
            pl.BlockSpec((tm, D), lambda i, j: (i, 0)),
            pl.BlockSpec((1, D), lambda i, j: (0, 0)),
            pl.BlockSpec((None, D, tn), lambda i, j: (l, 0, j)),
        ],
        out_specs=pl.BlockSpec((tm, tn), lambda i, j: (i, j)),
        out_shape=jax.ShapeDtypeStruct((M, N), f32),
        scratch_shapes=[pltpu.VMEM((tm, D), bf16)],
        compiler_params=_cparams(("parallel", "arbitrary")),
        name="norm_in_proj",
    )(x, gain.reshape(1, D), w)


def _cmul(ar, ai, br, bi):
    return ar * br - ai * bi, ar * bi + ai * br


def _s5_kernel(ua_ref, ub_ref, bexp_ref, cexp_ref, p1r_ref, p1i_ref, psr_ref, psi_ref, d_ref,
               h0p_ref, h0s_ref, wglu_ref, gain_ref, *rest,
               n_sets, lane_chunk, nt_prompt, n_seq_s, seq_s, n_cast):
    cast_src = rest[:n_cast]
    out_ref, hfp_ref, hfs_ref = rest[n_cast:n_cast + 3]
    cast_dst = rest[n_cast + 3:2 * n_cast + 3]
    up_ref, bu_ref, hb_ref, lam_ref, z_ref = rest[2 * n_cast + 3:]
    cast_refs = list(zip(cast_src, cast_dst))
    _s5_body(ua_ref, ub_ref, bexp_ref, cexp_ref, p1r_ref, p1i_ref, psr_ref, psi_ref, d_ref,
             h0p_ref, h0s_ref, wglu_ref, gain_ref, cast_refs,
             out_ref, hfp_ref, hfs_ref, up_ref, bu_ref, hb_ref, lam_ref, z_ref,
             n_sets=n_sets, lane_chunk=lane_chunk, nt_prompt=nt_prompt, n_seq_s=n_seq_s, seq_s=seq_s)


def _s5_body(ua_ref, ub_ref, bexp_ref, cexp_ref, p1r_ref, p1i_ref, psr_ref, psi_ref, d_ref,
             h0p_ref, h0s_ref, wglu_ref, gain_ref, cast_refs,
             out_ref, hfp_ref, hfs_ref, up_ref, bu_ref, hb_ref, lam_ref, z_ref,
             *, n_sets, lane_chunk, nt_prompt, n_seq_s, seq_s):
    i = pl.program_id(0)
    s = pl.program_id(1)
    tt, half = ua_ref.shape
    cw = 2 * half
    W = p1r_ref.shape[-1]
    LC = lane_chunk
    chunks = [c * LC for c in range(W // LC)]

    @pl.when(i == 0)
    def _():
        hfp_ref[0, s] = h0p_ref[0, s]

    lam_ref[0] = jnp.broadcast_to(p1r_ref[...], (SUBLANES, W))
    lam_ref[1] = jnp.broadcast_to(p1i_ref[...], (SUBLANES, W))

    def gather_rows(n_steps):
        def body(j, carry):
            dst = pl.ds(pl.multiple_of(j * SUBLANES, SUBLANES), SUBLANES)
            up_ref[dst, 0:half] = ua_ref[pl.ds(j, SUBLANES, stride=n_steps), :]
            up_ref[dst, half:cw] = ub_ref[pl.ds(j, SUBLANES, stride=n_steps), :]
            return carry
        lax.fori_loop(0, n_steps, body, 0, unroll=8)

    def in_proj(rows):
        hr = rows // 2
        for r0 in (0, hr):
            bu_ref[r0:r0 + hr, :] = jnp.dot(up_ref[r0:r0 + hr, :].astype(bf16), bexp_ref[...],
                                            preferred_element_type=f32)

    def advance(x, rows):
        new = []
        for c, lo in enumerate(chunks):
            pr, pi_ = _cmul(lam_ref[0, :, lo:lo + LC], lam_ref[1, :, lo:lo + LC], x[2 * c], x[2 * c + 1])
            new += [pr + bu_ref[rows, lo:lo + LC], pi_ + bu_ref[rows, W + lo:W + lo + LC]]
        return new

    def scan(n_steps, x0, store):
        def two_steps(jj, x):
            r0 = pl.multiple_of(jj * 2 * SUBLANES, 2 * SUBLANES)
            xa = advance(x, pl.ds(r0, SUBLANES))
            xb = advance(xa, pl.ds(r0 + SUBLANES, SUBLANES))
            if store:
                rows = pl.ds(r0, 2 * SUBLANES)
                for c, lo in enumerate(chunks):
                    hb_ref[rows, lo:lo + LC] = jnp.concatenate([xa[2 * c], xb[2 * c]], axis=0).astype(bf16)
                    hb_ref[rows, W + lo:W + lo + LC] = jnp.concatenate([xa[2 * c + 1], xb[2 * c + 1]], axis=0).astype(bf16)
            return tuple(xb)
        return lax.fori_loop(0, n_steps // 2, two_steps, tuple(x0), unroll=2)

    def finish(rows, n_steps):
        y = jnp.dot(hb_ref[0:rows, :], cexp_ref[...], preferred_element_type=f32) + d_ref[...] * up_ref[0:rows, :]
        up_ref[0:rows, :] = 0.5 * y * (1.0 + lax.erf(y * (1.0 / math.sqrt(2.0))))
        for src, dst in cast_refs:
            dst[...] = src[...].astype(bf16)

        def scatter(j, carry):
            src = pl.ds(pl.multiple_of(j * SUBLANES, SUBLANES), SUBLANES)
            z_ref[2 * s, pl.ds(j, SUBLANES, stride=n_steps), :] = up_ref[src, 0:half]
            z_ref[2 * s + 1, pl.ds(j, SUBLANES, stride=n_steps), :] = up_ref[src, half:cw]
            return carry
        lax.fori_loop(0, n_steps, scatter, 0, unroll=8)

        @pl.when(s == n_sets - 1)
        def _():
            z = jnp.concatenate([z_ref[k, 0:rows, :] for k in range(2 * n_sets)], axis=-1)
            gl = jnp.dot(z.astype(bf16), wglu_ref[...], preferred_element_type=f32)
            out_ref[0:rows, :] = _rms(z * jax.nn.sigmoid(gl), gain_ref[...]).astype(out_ref.dtype)

    @pl.when(i < nt_prompt)
    def _():
        S = tt // SUBLANES
        gather_rows(S)
        in_proj(tt)
        zero = jnp.zeros((SUBLANES, LC), f32)
        fin = scan(S, [zero] * (2 * len(chunks)), store=False)

        row = lax.broadcasted_iota(jnp.int32, (SUBLANES, LC), 0)
        c_in = hfp_ref[0, s]
        entry = []
        c_out_r, c_out_i = [], []
        for c, lo in enumerate(chunks):
            qr = psr_ref[:, lo:lo + LC]
            qi = psi_ref[:, lo:lo + LC]
            er, ei = fin[2 * c], fin[2 * c + 1]
            for d in (1, 2, 4):
                keep = row >= d
                tr = jnp.where(keep, jnp.broadcast_to(qr[d - 1:d], (SUBLANES, LC)), 0.0)
                ti = jnp.where(keep, jnp.broadcast_to(qi[d - 1:d], (SUBLANES, LC)), 0.0)
                pr, pi_ = _cmul(tr, ti, pltpu.roll(er, d, axis=0), pltpu.roll(ei, d, axis=0))
                er, ei = er + pr, ei + pi_
            cr = c_in[:, lo:lo + LC]
            ci = c_in[:, W + lo:W + lo + LC]
            first = row == 0
            gr = jnp.where(first, 1.0, pltpu.roll(qr, 1, axis=0))
            gi = jnp.where(first, 0.0, pltpu.roll(qi, 1, axis=0))
            pr, pi_ = _cmul(gr, gi, cr, ci)
            entry.append(jnp.where(first, 0.0, pltpu.roll(er, 1, axis=0)) + pr)
            entry.append(jnp.where(first, 0.0, pltpu.roll(ei, 1, axis=0)) + pi_)
            pr, pi_ = _cmul(qr[SUBLANES - 1:SUBLANES], qi[SUBLANES - 1:SUBLANES], cr, ci)
            c_out_r.append(er[SUBLANES - 1:SUBLANES] + pr)
            c_out_i.append(ei[SUBLANES - 1:SUBLANES] + pi_)
        hfp_ref[0, s] = jnp.concatenate(c_out_r + c_out_i, axis=-1)

        scan(S, entry, store=True)
        finish(tt, S)

    @pl.when(i == nt_prompt)
    def _():
        rows = n_seq_s * seq_s
        gather_rows(seq_s)
        in_proj(rows)
        h0 = h0s_ref[s]
        x0 = []
        for lo in chunks:
            x0 += [h0[:, lo:lo + LC], h0[:, W + lo:W + lo + LC]]
        fin = scan(seq_s, x0, store=True)
        hfs_ref[s] = jnp.concatenate([fin[2 * c] for c in range(len(chunks))]
                                     + [fin[2 * c + 1] for c in range(len(chunks))], axis=-1)
        finish(rows, seq_s)


def _s5_group(proj, bexp, cexp, p1r, p1i, psr, psi, d, h0p, h0s, wglu, gain, l, n_prompt, seq_s, casts=()):
    M = proj.shape[0]
    _, n_sets, cw, w2 = bexp.shape
    W = w2 // 2
    half = cw // 2
    tt = SEQ_TILE
    nt_prompt = n_prompt // tt
    n_seq_s = h0s.shape[1]
    assert half == LANES and n_seq_s == SUBLANES and seq_s % 4 == 0 and (tt // SUBLANES) % 4 == 0
    assert n_prompt % tt == 0 and n_seq_s * seq_s <= tt and M == n_prompt + n_seq_s * seq_s
    s5w = n_sets * cw
    kern = functools.partial(_s5_kernel, n_sets=n_sets, lane_chunk=min(256, W), nt_prompt=nt_prompt,
                             n_seq_s=n_seq_s, seq_s=seq_s, n_cast=len(casts))

    cast_in, cast_out, cast_shapes = [], [], []
    for w, wl in casts:
        _, rows_w, cols_w = w.shape
        n_panels = cols_w // LANES
        assert cols_w % LANES == 0 and (nt_prompt + 1) * n_sets >= n_panels

        def panel(i, s, n_panels=n_panels):
            return jnp.minimum(i * n_sets + s, n_panels - 1)

        cast_in.append(pl.BlockSpec((None, rows_w, LANES), lambda i, s, wl=wl, panel=panel: (wl, 0, panel(i, s))))
        cast_out.append(pl.BlockSpec((None, rows_w, LANES), lambda i, s, panel=panel: (0, 0, panel(i, s))))
        cast_shapes.append(jax.ShapeDtypeStruct((1, rows_w, cols_w), bf16))

    return pl.pallas_call(
        kern,
        grid=(nt_prompt + 1, n_sets),
        in_specs=[
            pl.BlockSpec((tt, half), lambda i, s: (i, 2 * s)),
            pl.BlockSpec((tt, half), lambda i, s: (i, 2 * s + 1)),
            pl.BlockSpec((None, None, cw, w2), lambda i, s: (l, s, 0, 0)),
            pl.BlockSpec((None, None, w2, cw), lambda i, s: (l, s, 0, 0)),
            pl.BlockSpec((None, 1, W), lambda i, s: (l, 0, s)),
            pl.BlockSpec((None, 1, W), lambda i, s: (l, 0, s)),
            pl.BlockSpec((None, SUBLANES, W), lambda i, s: (l, 0, s)),
            pl.BlockSpec((None, SUBLANES, W), lambda i, s: (l, 0, s)),
            pl.BlockSpec((None, None, 1, cw), lambda i, s: (l, s, 0, 0)),
            pl.BlockSpec((1, n_sets, 1, w2), lambda i, s: (0, 0, 0, 0)),
            pl.BlockSpec((n_sets, n_seq_s, w2), lambda i, s: (0, 0, 0)),
            pl.BlockSpec((None, s5w, s5w), lambda i, s: (l, 0, 0), pipeline_mode=pl.Buffered(1)),
            pl.BlockSpec((1, s5w), lambda i, s: (0, 0)),
        ] + cast_in,
        out_specs=[
            pl.BlockSpec((tt, s5w), lambda i, s: (i, 0)),
            pl.BlockSpec((1, n_sets, 1, w2), lambda i, s: (0, 0, 0, 0)),
            pl.BlockSpec((n_sets, n_seq_s, w2), lambda i, s: (0, 0, 0)),
        ] + cast_out,
        out_shape=[
            jax.ShapeDtypeStruct((M, s5w), bf16),
            jax.ShapeDtypeStruct((1, n_sets, 1, w2), f32),
            jax.ShapeDtypeStruct((n_sets, n_seq_s, w2), f32),
        ] + cast_shapes,
        scratch_shapes=[
            pltpu.VMEM((tt, cw), f32),
            pltpu.VMEM((tt, w2), f32),
            pltpu.VMEM((tt, w2), bf16),
            pltpu.VMEM((2, SUBLANES, W), f32),
            pltpu.VMEM((2 * n_sets, tt, half), f32),
        ],
        compiler_params=_cparams(("arbitrary", "arbitrary")),
        name="s5_group",
    )(proj, proj, bexp, cexp, p1r, p1i, psr, psi, d, h0p, h0s, wglu, gain, *[w for w, _ in casts])


def _pair_levels(C):
    t = np.arange(C)[:, None]
    s = np.arange(C)[None, :]
    lvl = np.full((C, C), -1, np.int32)
    m, n = 1, 0
    while m < C:
        own = (t // (2 * m) == s // (2 * m)) & ((t // m) % 2 == 1) & ((s // m) % 2 == 0)
        lvl[own] = n
        m *= 2
        n += 1
    return lvl


def _cumsum_rows(x):
    C = x.shape[0]
    nb = C // SUBLANES
    x3 = x.reshape(nb, SUBLANES, x.shape[1])
    row = lax.broadcasted_iota(jnp.int32, (1, SUBLANES, x.shape[1]), 1)
    for d in (1, 2, 4):
        x3 = x3 + jnp.where(row >= d, pltpu.roll(x3, d, axis=1), 0.0)
    outs = [x3[0:1]]
    acc = x3[0:1, SUBLANES - 1:SUBLANES, :]
    for b in range(1, nb):
        outs.append(x3[b:b + 1] + acc)
        acc = acc + x3[b:b + 1, SUBLANES - 1:SUBLANES, :]
    return jnp.concatenate(outs, axis=0).reshape(C, x.shape[1])


def _dot_nt(a, b):
    return lax.dot_general(a, b, (((1,), (1,)), ((), ())), preferred_element_type=f32)


def _dot_tn(a, b):
    return lax.dot_general(a, b, (((0,), (0,)), ((), ())), preferred_element_type=f32)


def _hgrn_chunk(q, z, v, g, lb, gain, st, lvl):
    C, K = q.shape
    nb = C // SUBLANES
    qf = q * jax.nn.sigmoid(q)
    e = jnp.exp(-jnp.abs(z))
    r = 1.0 / (1.0 + e)
    er = e * r
    pos = z >= 0.0
    om = 1.0 - lb
    fc = jnp.maximum(lb + om * jnp.where(pos, r, er), F_TINY)
    logf = jnp.log(fc)
    kf = om * jnp.where(pos, er, r)
    bc = _cumsum_rows(logf)
    blast = bc[C - 1:C, :]
    vb = v.astype(bf16)

    o = _dot_nt((qf * jnp.exp(bc)).astype(bf16), st.astype(bf16))
    st_new = st * jnp.exp(blast) + _dot_tn(vb, (kf * jnp.exp(blast - bc)).astype(bf16))

    o = o + jnp.sum(qf * kf, axis=-1, keepdims=True) * v

    q3 = qf.reshape(nb, SUBLANES, K)
    k3 = kf.reshape(nb, SUBLANES, K)
    b3 = bc.reshape(nb, SUBLANES, K)
    sub = lax.broadcasted_iota(jnp.int32, (1, SUBLANES, K), 1)
    att = None
    m, n = 1, 0
    while m < C:
        if m == 1:
            upper = (sub & 1) == 1
            qm = jnp.where(upper, q3 * fc.reshape(nb, SUBLANES, K), 0.0).reshape(C, K)
            km = jnp.where(upper, 0.0, k3).reshape(C, K)
        elif m < SUBLANES:
            bref = jnp.broadcast_to(b3[:, m - 1:m, :], b3.shape)
            for base in range(2 * m, SUBLANES, 2 * m):
                bref = jnp.where(sub >= base, jnp.broadcast_to(b3[:, base + m - 1:base + m, :], b3.shape), bref)
            dec = jnp.exp(-jnp.abs(b3 - bref))
            upper = (sub & m) == m
            qm = jnp.where(upper, q3 * dec, 0.0).reshape(C, K)
            km = jnp.where(upper, 0.0, k3 * dec).reshape(C, K)
        else:
            nblk = C // (2 * m)
            b4 = bc.reshape(nblk, 2, m, K)
            dec = jnp.exp(-jnp.abs(b4 - b4[:, 0:1, m - 1:m, :]))
            zero = jnp.zeros((nblk, 1, m, K), f32)
            qm = jnp.concatenate([zero, qf.reshape(nblk, 2, m, K)[:, 1:2] * dec[:, 1:2]], axis=1).reshape(C, K)
            km = jnp.concatenate([kf.reshape(nblk, 2, m, K)[:, 0:1] * dec[:, 0:1], zero], axis=1).reshape(C, K)
        a = _dot_nt(qm.astype(bf16), km.astype(bf16))
        att = a if att is None else jnp.where(lvl == n, a, att)
        m *= 2
        n += 1
    att = jnp.where(lvl >= 0, att, 0.0)
    o = o + jnp.dot(att.astype(bf16), vb, preferred_element_type=f32)

    out = _rms(o, gain) * (g * jax.nn.sigmoid(g))
    return out, st_new


def _hgrn_kernel(q_ref, z_ref, v_ref, g_ref, lvlp_ref, lvls_ref, lb_ref, gain_ref, s0p_ref, s0s_ref,
                 wg_ref, wu_ref, wd_ref,
                 out_ref, sfp_ref, sfs_ref, wgb_ref, wub_ref, wdb_ref, st_ref,
                 *, heads, chunk_p, nt_prompt, n_seq_s, seq_s):
    i = pl.program_id(1)
    tt = q_ref.shape[0]
    K = LANES
    gain = gain_ref[...]
    n_cast = tt // chunk_p

    def cast_part(c):
        def part(n):
            return slice(c * n, (c + 1) * n) if isinstance(c, int) else pl.ds(pl.multiple_of(c * n, n), n)
        rows = part(wg_ref.shape[0] // n_cast)
        wgb_ref[rows, :] = wg_ref[rows, :].astype(bf16)
        wub_ref[rows, :] = wu_ref[rows, :].astype(bf16)
        rows = part(wd_ref.shape[0] // n_cast)
        wdb_ref[rows, :] = wd_ref[rows, :].astype(bf16)

    def chunk(rows, h, st, lvl):
        ls = slice(h * K, (h + 1) * K)
        out, st_new = _hgrn_chunk(q_ref[rows, ls], z_ref[rows, ls], v_ref[rows, ls], g_ref[rows, ls],
                                  lb_ref[:, ls], gain, st, lvl)
        out_ref[rows, ls] = out.astype(out_ref.dtype)
        return st_new

    @pl.when(i == 0)
    def _():
        for h in range(heads):
            st_ref[h] = s0p_ref[0, h].T

    @pl.when(i < nt_prompt)
    def _():
        def body(c, carry):
            rows = pl.ds(pl.multiple_of(c * chunk_p, chunk_p), chunk_p)
            lvl = lvlp_ref[...]
            for h in range(heads):
                st_ref[h] = chunk(rows, h, st_ref[h], lvl)
            cast_part(c)
            return carry

        lax.fori_loop(0, tt // chunk_p, body, 0, unroll=2)

    @pl.when(i == nt_prompt - 1)
    def _():
        for h in range(heads):
            sfp_ref[0, h] = st_ref[h].T

    @pl.when(i == nt_prompt)
    def _():
        lvl = lvls_ref[...]
        for b in range(n_seq_s):
            for h in range(heads):
                sfs_ref[b, h] = chunk(slice(b * seq_s, (b + 1) * seq_s), h, s0s_ref[b, h].T, lvl).T
        for c in range(n_cast):
            cast_part(c)


def _hgrn_group(proj, lbs, gain, s0p, s0s, wg, wu, wd, l, n_prompt, seq_s, col0):
    M = proj.shape[0]
    _, D, F = wg.shape
    n_panels = F // CAST_BLOCK
    n_seq_s, H, K, V = s0s.shape
    assert K == LANES and V == LANES and s0p.shape[0] == 1
    heads = HG_HEADS_PER_STEP
    tt = SEQ_TILE
    nt_prompt = n_prompt // tt
    chunk_p = min(HG_CHUNK, n_prompt)
    assert seq_s <= HG_CHUNK and n_prompt % tt == 0 and tt % chunk_p == 0 and n_seq_s * seq_s <= tt
    assert M == n_prompt + n_seq_s * seq_s
    cb = col0 // (heads * K)
    hb = H // heads
    kern = functools.partial(_hgrn_kernel, heads=heads, chunk_p=chunk_p, nt_prompt=nt_prompt,
                             n_seq_s=n_seq_s, seq_s=seq_s)

    def col_spec(k):
        return pl.BlockSpec((tt, heads * K), lambda hp, i, k=k: (i, cb + k * hb + hp))

    n_steps = nt_prompt + 1
    assert hb * n_steps >= n_panels and F % CAST_BLOCK == 0

    def panel(hp, i):
        return jnp.minimum(hp * n_steps + i, n_panels - 1)

    return pl.pallas_call(
        kern,
        grid=(hb, n_steps),
        in_specs=[
            col_spec(0), col_spec(1), col_spec(2), col_spec(3),
            pl.BlockSpec((chunk_p, chunk_p), lambda hp, i: (0, 0)),
            pl.BlockSpec((seq_s, seq_s), lambda hp, i: (0, 0)),
            pl.BlockSpec((1, heads * K), lambda hp, i: (0, hp)),
            pl.BlockSpec((1, V), lambda hp, i: (0, 0)),
            pl.BlockSpec((1, heads, K, V), lambda hp, i: (0, hp, 0, 0)),
            pl.BlockSpec((n_seq_s, heads, K, V), lambda hp, i: (0, hp, 0, 0), pipeline_mode=pl.Buffered(1)),
            pl.BlockSpec((None, D, CAST_BLOCK), lambda hp, i: (l, 0, panel(hp, i))),
            pl.BlockSpec((None, D, CAST_BLOCK), lambda hp, i: (l, 0, panel(hp, i))),
            pl.BlockSpec((None, CAST_BLOCK, D), lambda hp, i: (l, panel(hp, i), 0)),
        ],
        out_specs=[
            pl.BlockSpec((tt, heads * V), lambda hp, i: (i, hp)),
            pl.BlockSpec((1, heads, K, V), lambda hp, i: (0, hp, 0, 0)),
            pl.BlockSpec((n_seq_s, heads, K, V), lambda hp, i: (0, hp, 0, 0)),
            pl.BlockSpec((None, D, CAST_BLOCK), lambda hp, i: (0, 0, panel(hp, i))),
            pl.BlockSpec((None, D, CAST_BLOCK), lambda hp, i: (0, 0, panel(hp, i))),
            pl.BlockSpec((None, CAST_BLOCK, D), lambda hp, i: (0, panel(hp, i), 0)),
        ],
        out_shape=[
            jax.ShapeDtypeStruct((M, H * V), bf16),
            jax.ShapeDtypeStruct((1, H, K, V), f32),
            jax.ShapeDtypeStruct((n_seq_s, H, K, V), f32),
            jax.ShapeDtypeStruct((1, D, F), bf16),
            jax.ShapeDtypeStruct((1, D, F), bf16),
            jax.ShapeDtypeStruct((1, F, D), bf16),
        ],
        scratch_shapes=[pltpu.VMEM((heads, V, K), f32)],
        compiler_params=_cparams(("arbitrary", "arbitrary")),
        name="hgrn2_group",
    )(proj, proj, proj, proj, jnp.asarray(_pair_levels(chunk_p)), jnp.asarray(_pair_levels(seq_s)),
      lbs, gain, s0p, s0s, wg, wu, wd)


def _out_proj_kernel(x_ref, a_ref, b_ref, wa_ref, wb_ref, o_ref):
    acc = jnp.dot(a_ref[...], wa_ref[...], preferred_element_type=f32)
    acc = acc + jnp.dot(b_ref[...], wb_ref[...], preferred_element_type=f32)
    o_ref[...] = x_ref[...] + acc


def _out_proj(x, a, b, w, l, tm, tn):
    M, D = x.shape
    ka = a.shape[1]
    kb = b.shape[1]
    assert ka == kb
    return pl.pallas_call(
        _out_proj_kernel,
        grid=(M // tm, D // tn),
        in_specs=[
            pl.BlockSpec((tm, tn), lambda i, j: (i, j)),
            pl.BlockSpec((tm, ka), lambda i, j: (i, 0)),
            pl.BlockSpec((tm, kb), lambda i, j: (i, 0)),
            pl.BlockSpec((None, ka, tn), lambda i, j: (l, 0, j)),
            pl.BlockSpec((None, kb, tn), lambda i, j: (l, 1, j)),
        ],
        out_specs=pl.BlockSpec((tm, tn), lambda i, j: (i, j)),
        out_shape=jax.ShapeDtypeStruct((M, D), f32),
        compiler_params=_cparams(("parallel", "arbitrary")),
        name="out_proj",
    )(x, a, b, w, w)


def _ffn_kernel(x_ref, g_ref, wg_ref, wu_ref, wd_ref, gf_ref, o_ref, h_ref, *, final_norm):
    j = pl.program_id(1)

    tm, D = o_ref.shape

    def row_blocks(fn):
        def body(r, carry):
            fn(pl.ds(pl.multiple_of(r * ROW_BLOCK, ROW_BLOCK), ROW_BLOCK))
            return carry
        lax.fori_loop(0, tm // ROW_BLOCK, body, 0)

    @pl.when(j == 0)
    def _():
        def init(rows):
            x = x_ref[rows, :]
            h_ref[rows, :] = _rms(x, g_ref[...]).astype(bf16)
            o_ref[rows, :] = x
        row_blocks(init)

    h = h_ref[...]
    gate = jnp.dot(h, wg_ref[...], preferred_element_type=f32)
    up = jnp.dot(h, wu_ref[...], preferred_element_type=f32)
    act = (gate * jax.nn.sigmoid(gate) * up).astype(bf16)
    for n in range(0, D, COL_PANEL):
        o_ref[:, n:n + COL_PANEL] += jnp.dot(act, wd_ref[:, n:n + COL_PANEL], preferred_element_type=f32)

    if final_norm:
        @pl.when(j == pl.num_programs(1) - 1)
        def _():
            def close(rows):
                o_ref[rows, :] = _rms(o_ref[rows, :], gf_ref[...])
            row_blocks(close)


def _ffn(x, gain, wg, wu, wd, gain_final, final_norm, l, tm, tf):
    M, D = x.shape
    F = wg.shape[2]
    kern = functools.partial(_ffn_kernel, final_norm=final_norm)
    return pl.pallas_call(
        kern,
        grid=(M // tm, F // tf),
        in_specs=[
            pl.BlockSpec((tm, D), lambda i, j: (i, 0), pipeline_mode=pl.Buffered(1)),
            pl.BlockSpec((1, D), lambda i, j: (0, 0)),
            pl.BlockSpec((None, D, tf), lambda i, j: (l, 0, j)),
            pl.BlockSpec((None, D, tf), lambda i, j: (l, 0, j)),
            pl.BlockSpec((None, tf, D), lambda i, j: (l, j, 0)),
            pl.BlockSpec((1, D), lambda i, j: (0, 0)),
        ],
        out_specs=pl.BlockSpec((tm, D), lambda i, j: (i, 0)),
        out_shape=jax.ShapeDtypeStruct((M, D), f32),
        scratch_shapes=[pltpu.VMEM((tm, D), bf16)],
        compiler_params=_cparams(("parallel", "arbitrary")),
        name="swiglu_ffn",
    )(x, gain.reshape(1, D), wg, wu, wd, gain_final.reshape(1, D))


def _expand_s5_weights(bb_re, bb_im, c_re, c_im):
    L, G, P, H = bb_re.shape
    ns = G // S5_SET_GROUPS
    grp_of_row = np.arange(S5_SET_GROUPS * H) // H
    grp_of_col = np.arange(S5_SET_GROUPS * P) // P
    same_group = jnp.asarray(grp_of_row[:, None] == grp_of_col[None, :], f32)

    def expand_b(bb):
        t = bb.reshape(L, ns, S5_SET_GROUPS, P, H).transpose(0, 1, 2, 4, 3).reshape(L, ns, S5_SET_GROUPS * H, P)
        return jnp.tile(t, (1, 1, 1, S5_SET_GROUPS)) * same_group

    def expand_c(c):
        t = c.reshape(L, ns, S5_SET_GROUPS * H, P).transpose(0, 1, 3, 2)
        return jnp.tile(t, (1, 1, S5_SET_GROUPS, 1)) * same_group.T

    bexp = jnp.concatenate([expand_b(bb_re), expand_b(bb_im)], axis=-1).astype(bf16)
    cexp = jnp.concatenate([expand_c(c_re), -expand_c(c_im)], axis=-2).astype(bf16)
    return bexp, cexp


def kernel(x_prompt, x_sample, state_s5_re, state_s5_im, state_hgrn, w_in, w_out, norm_mix, norm_ffn, norm_final, s5_a_re, s5_a_im, s5_log_dt, s5_b_re, s5_b_im, s5_c_re, s5_c_im, s5_d, s5_w_glu, s5_out_norm, hg_lower_bounds, hg_out_norm, ffn_w_gate, ffn_w_up, ffn_w_down):
    L, G, P = s5_a_re.shape
    Hs = s5_b_re.shape[-1]
    _, _, HH, HK, HV = state_hgrn.shape
    PB, PT, D = x_prompt.shape
    SB, ST, _ = x_sample.shape
    assert PB == 1
    ns = G // S5_SET_GROUPS
    W = S5_SET_GROUPS * P
    s5w = G * Hs
    n_prompt = PB * PT
    M = n_prompt + SB * ST
    assert M % ROW_TILE == 0

    p1r, p1i, psr, psi, bb_re, bb_im, lbs = _prep(s5_a_re, s5_a_im, s5_log_dt, s5_b_re, s5_b_im, hg_lower_bounds)
    bexp, cexp = _expand_s5_weights(bb_re, bb_im, s5_c_re, s5_c_im)
    d_sets = s5_d.reshape(L, ns, 1, S5_SET_GROUPS * Hs)

    w_in_b = w_in[0:1].astype(bf16)
    w_glu_b = s5_w_glu.astype(bf16)

    zero_s5 = jnp.zeros((PB, ns, 1, 2 * W), f32)
    zero_hg = jnp.zeros((PB, HH, HK, HV), f32)

    x = jnp.concatenate([x_prompt.reshape(n_prompt, D), x_sample.reshape(SB * ST, D)], axis=0)
    p_re, p_im, p_hg, s_re, s_im, s_hg = [], [], [], [], [], []
    for l in range(L):
        h0s = jnp.concatenate([state_s5_re[l].reshape(SB, ns, W), state_s5_im[l].reshape(SB, ns, W)],
                              axis=-1).transpose(1, 0, 2)
        proj = _norm_matmul(x, norm_mix[l], w_in_b, 0, ROW_TILE, 1024)
        casts = [(w_out, l)] + ([(w_in, l + 1)] if l + 1 < L else [])
        s5_out, hfp, hfs, w_out_b, *w_in_next = _s5_group(
            proj, bexp, cexp, p1r, p1i, psr, psi, d_sets, zero_s5, h0s, w_glu_b,
            s5_out_norm[l].reshape(1, s5w), l, n_prompt, ST, casts)
        if w_in_next:
            w_in_b = w_in_next[0]
        hfs = hfs.transpose(1, 0, 2)
        hg_out, sfp, sfs, wg_b, wu_b, wd_b = _hgrn_group(
            proj, lbs[l].reshape(1, HH * HK), hg_out_norm[l].reshape(1, HV), zero_hg, state_hgrn[l],
            ffn_w_gate, ffn_w_up, ffn_w_down, l, n_prompt, ST, s5w)
        x = _out_proj(x, s5_out, hg_out, w_out_b, 0, ROW_TILE, 1024)
        x = _ffn(x, norm_ffn[l], wg_b, wu_b, wd_b, norm_final, l == L - 1, 0, ROW_TILE, 256)
        p_re.append(hfp[..., :W].reshape(PB, G, P))
        p_im.append(hfp[..., W:].reshape(PB, G, P))
        p_hg.append(sfp)
        s_re.append(hfs[..., :W].reshape(SB, G, P))
        s_im.append(hfs[..., W:].reshape(SB, G, P))
        s_hg.append(sfs)
    y_prompt = x[:n_prompt].reshape(PB, PT, D)
    y_sample = x[n_prompt:].reshape(SB, ST, D)
    return (y_prompt, y_sample, jnp.stack(p_re), jnp.stack(p_im), jnp.stack(p_hg),
            jnp.stack(s_re), jnp.stack(s_im), jnp.stack(s_hg))
```

```python
import functools
import math

import jax
import jax.numpy as jnp
import numpy as np
from jax import lax
from jax.experimental import pallas as pl
from jax.experimental.pallas import tpu as pltpu

f32 = jnp.float32
bf16 = jnp.bfloat16

EPS = 1e-6
F_TINY = 1e-30
HG_CHUNK = 128
SUBLANES = 8
LANES = 128
S5_SET_GROUPS = 16
HG_HEADS_PER_STEP = 4
VMEM_LIMIT = 56 * 1024 * 1024
ROW_TILE = 640
SEQ_TILE = 512
ROW_BLOCK = 64
COL_PANEL = 512
CAST_BLOCK = 256


def _cparams(sem):
    return pltpu.CompilerParams(dimension_semantics=sem, vmem_limit_bytes=VMEM_LIMIT)


def _rms(x, gain):
    return x * lax.rsqrt(jnp.mean(x * x, axis=-1, keepdims=True) + EPS) * gain


def _prep_kernel(are_ref, aim_ref, ldt_ref, arer_ref, aimr_ref, ldtr_ref, bre_ref, bim_ref, lb_ref,
                 p1r_ref, p1i_ref, pseg_r_ref, pseg_i_ref, bbr_ref, bbi_ref, lbs_ref):
    S = SEQ_TILE // SUBLANES
    GP = p1r_ref.shape[-1]
    dt = jnp.exp(ldt_ref[...])
    adt_re = are_ref[...] * dt
    adt_im = aim_ref[...] * dt
    mag = jnp.exp(adt_re)
    p1r_ref[...] = mag * jnp.cos(adt_im)
    p1i_ref[...] = mag * jnp.sin(adt_im)
    ks = ((lax.broadcasted_iota(jnp.int32, (SUBLANES, GP), 0) + 1) * S).astype(f32)
    mag = jnp.exp(adt_re * ks)
    pseg_r_ref[...] = mag * jnp.cos(adt_im * ks)
    pseg_i_ref[...] = mag * jnp.sin(adt_im * ks)
    a_re = arer_ref[...]
    a_im = aimr_ref[...]
    dt = jnp.exp(ldtr_ref[...])
    mag = jnp.exp(a_re * dt)
    lam_re = mag * jnp.cos(a_im * dt)
    lam_im = mag * jnp.sin(a_im * dt)
    den = a_re * a_re + a_im * a_im
    coef_re = ((lam_re - 1.0) * a_re + lam_im * a_im) / den
    coef_im = (lam_im * a_re - (lam_re - 1.0) * a_im) / den
    b_re = bre_ref[...]
    b_im = bim_ref[...]
    bbr_ref[...] = coef_re * b_re - coef_im * b_im
    bbi_ref[...] = coef_re * b_im + coef_im * b_re
    lb = lb_ref[...]
    e = jnp.exp(lb - jnp.max(lb, axis=0, keepdims=True))
    sm = e / jnp.sum(e, axis=0, keepdims=True)
    acc = jnp.zeros_like(sm[0:1])
    for l in range(lb.shape[0]):
        acc = acc + sm[l:l + 1]
        lbs_ref[l:l + 1, :] = acc - sm[0:1]


def _prep(a_re, a_im, log_dt, b_re, b_im, lower_bounds):
    L, G, P = a_re.shape
    H = b_re.shape[-1]
    GP = G * P
    PH = P * H
    are = a_re.reshape(L, 1, GP)
    aim = a_im.reshape(L, 1, GP)
    ldt = jnp.repeat(log_dt, P, axis=-1).reshape(L, 1, GP)
    arer = jnp.repeat(a_re.reshape(L * G, P), H, axis=-1)
    aimr = jnp.repeat(a_im.reshape(L * G, P), H, axis=-1)
    ldtr = jnp.broadcast_to(log_dt.reshape(L * G, 1), (L * G, PH))
    row_spec = pl.BlockSpec((None, 1, GP), lambda l: (l, 0, 0))
    grp_spec = pl.BlockSpec((G, PH), lambda l: (l, 0))
    lb_spec = pl.BlockSpec(lower_bounds.shape, lambda l: (0, 0))
    out_shape = (
        jax.ShapeDtypeStruct((L, 1, GP), f32),
        jax.ShapeDtypeStruct((L, 1, GP), f32),
        jax.ShapeDtypeStruct((L, SUBLANES, GP), f32),
        jax.ShapeDtypeStruct((L, SUBLANES, GP), f32),
        jax.ShapeDtypeStruct((L * G, PH), f32),
        jax.ShapeDtypeStruct((L * G, PH), f32),
        jax.ShapeDtypeStruct(lower_bounds.shape, f32),
    )
    p1r, p1i, psr, psi, bbr, bbi, lbs = pl.pallas_call(
        _prep_kernel,
        grid=(L,),
        in_specs=[row_spec, row_spec, row_spec, grp_spec, grp_spec, grp_spec, grp_spec, grp_spec, lb_spec],
        out_specs=[
            row_spec,
            row_spec,
            pl.BlockSpec((None, SUBLANES, GP), lambda l: (l, 0, 0)),
            pl.BlockSpec((None, SUBLANES, GP), lambda l: (l, 0, 0)),
            grp_spec, grp_spec, lb_spec,
        ],
        out_shape=out_shape,
        compiler_params=_cparams(("arbitrary",)),
        name="param_prep",
    )(are, aim, ldt, arer, aimr, ldtr, b_re.reshape(L * G, PH), b_im.reshape(L * G, PH), lower_bounds)
    return p1r, p1i, psr, psi, bbr.reshape(L, G, P, H), bbi.reshape(L, G, P, H), lbs


def _norm_matmul_kernel(x_ref, g_ref, w_ref, o_ref, h_ref):
    @pl.when(pl.program_id(1) == 0)
    def _():
        def body(r, carry):
            rows = pl.ds(pl.multiple_of(r * ROW_BLOCK, ROW_BLOCK), ROW_BLOCK)
            h_ref[rows, :] = _rms(x_ref[rows, :], g_ref[...]).astype(bf16)
            return carry
        lax.fori_loop(0, x_ref.shape[0] // ROW_BLOCK, body, 0)

    o_ref[...] = jnp.dot(h_ref[...], w_ref[...], preferred_element_type=f32)


def _norm_matmul(x, gain, w, l, tm, tn):
    M, D = x.shape
    N = w.shape[2]
    return pl.pallas_call(
        _norm_matmul_kernel,
        grid=(M // tm, N // tn),
        in_specs=[
            pl.BlockSpec((tm, D), lambda i, j: (i, 0)),
            pl.BlockSpec((1, D), lambda i, j: (0, 0)),
            pl.BlockSpec((None, D, tn), lambda i, j: (l, 0, j)),
        ],
        out_specs=pl.BlockSpec((tm, tn), lambda i, j: (i, j)),
        out_shape=jax.ShapeDtypeStruct((M, N), f32),
        scratch_shapes=[pltpu.VMEM((tm, D), bf16)],
        compiler_params=_cparams(("parallel", "arbitrary")),
        name="norm_in_proj",
    )(x, gain.reshape(1, D), w)


def _cmul(ar, ai, br, bi):
    return ar * br - ai * bi, ar * bi + ai * br


def _s5_kernel(ua_ref, ub_ref, bexp_ref, cexp_ref, p1r_ref, p1i_ref, psr_ref, psi_ref, d_ref,
               h0p_ref, h0s_ref, wglu_ref, gain_ref, *rest,
               n_sets, lane_chunk, nt_prompt, n_seq_s, seq_s, n_cast):
    cast_src = rest[:n_cast]
    out_ref, hfp_ref, hfs_ref = rest[n_cast:n_cast + 3]
    cast_dst = rest[n_cast + 3:2 * n_cast + 3]
    up_ref, bu_ref, hb_ref, lam_ref, z_ref = rest[2 * n_cast + 3:]
    cast_refs = list(zip(cast_src, cast_dst))
    _s5_body(ua_ref, ub_ref, bexp_ref, cexp_ref, p1r_ref, p1i_ref, psr_ref, psi_ref, d_ref,
             h0p_ref, h0s_ref, wglu_ref, gain_ref, cast_refs,
             out_ref, hfp_ref, hfs_ref, up_ref, bu_ref, hb_ref, lam_ref, z_ref,
             n_sets=n_sets, lane_chunk=lane_chunk, nt_prompt=nt_prompt, n_seq_s=n_seq_s, seq_s=seq_s)


def _s5_body(ua_ref, ub_ref, bexp_ref, cexp_ref, p1r_ref, p1i_ref, psr_ref, psi_ref, d_ref,
             h0p_ref, h0s_ref, wglu_ref, gain_ref, cast_refs,
             out_ref, hfp_ref, hfs_ref, up_ref, bu_ref, hb_ref, lam_ref, z_ref,
             *, n_sets, lane_chunk, nt_prompt, n_seq_s, seq_s):
    i = pl.program_id(0)
    s = pl.program_id(1)
    tt, half = ua_ref.shape
    cw = 2 * half
    W = p1r_ref.shape[-1]
    LC = lane_chunk
    chunks = [c * LC for c in range(W // LC)]

    @pl.when(i == 0)
    def _():
        hfp_ref[0, s] = h0p_ref[0, s]

    lam_ref[0] = jnp.broadcast_to(p1r_ref[...], (SUBLANES, W))
    lam_ref[1] = jnp.broadcast_to(p1i_ref[...], (SUBLANES, W))

    def gather_rows(n_steps):
        def body(j, carry):
            dst = pl.ds(pl.multiple_of(j * SUBLANES, SUBLANES), SUBLANES)
            up_ref[dst, 0:half] = ua_ref[pl.ds(j, SUBLANES, stride=n_steps), :]
            up_ref[dst, half:cw] = ub_ref[pl.ds(j, SUBLANES, stride=n_steps), :]
            return carry
        lax.fori_loop(0, n_steps, body, 0, unroll=8)

    def in_proj(rows):
        hr = rows // 2
        for r0 in (0, hr):
            bu_ref[r0:r0 + hr, :] = jnp.dot(up_ref[r0:r0 + hr, :].astype(bf16), bexp_ref[...],
                                            preferred_element_type=f32)

    def advance(x, rows):
        new = []
        for c, lo in enumerate(chunks):
            pr, pi_ = _cmul(lam_ref[0, :, lo:lo + LC], lam_ref[1, :, lo:lo + LC], x[2 * c], x[2 * c + 1])
            new += [pr + bu_ref[rows, lo:lo + LC], pi_ + bu_ref[rows, W + lo:W + lo + LC]]
        return new

    def scan(n_steps, x0, store):
        def two_steps(jj, x):
            r0 = pl.multiple_of(jj * 2 * SUBLANES, 2 * SUBLANES)
            xa = advance(x, pl.ds(r0, SUBLANES))
            xb = advance(xa, pl.ds(r0 + SUBLANES, SUBLANES))
            if store:
                rows = pl.ds(r0, 2 * SUBLANES)
                for c, lo in enumerate(chunks):
                    hb_ref[rows, lo:lo + LC] = jnp.concatenate([xa[2 * c], xb[2 * c]], axis=0).astype(bf16)
                    hb_ref[rows, W + lo:W + lo + LC] = jnp.concatenate([xa[2 * c + 1], xb[2 * c + 1]], axis=0).astype(bf16)
            return tuple(xb)
        return lax.fori_loop(0, n_steps // 2, two_steps, tuple(x0), unroll=2)

    def finish(rows, n_steps):
        y = jnp.dot(hb_ref[0:rows, :], cexp_ref[...], preferred_element_type=f32) + d_ref[...] * up_ref[0:rows, :]
        up_ref[0:rows, :] = 0.5 * y * (1.0 + lax.erf(y * (1.0 / math.sqrt(2.0))))
        for src, dst in cast_refs:
            dst[...] = src[...].astype(bf16)

        def scatter(j, carry):
            src = pl.ds(pl.multiple_of(j * SUBLANES, SUBLANES), SUBLANES)
            z_ref[2 * s, pl.ds(j, SUBLANES, stride=n_steps), :] = up_ref[src, 0:half]
            z_ref[2 * s + 1, pl.ds(j, SUBLANES, stride=n_steps), :] = up_ref[src, half:cw]
            return carry
        lax.fori_loop(0, n_steps, scatter, 0, unroll=8)

        @pl.when(s == n_sets - 1)
        def _():
            z = jnp.concatenate([z_ref[k, 0:rows, :] for k in range(2 * n_sets)], axis=-1)
            gl = jnp.dot(z.astype(bf16), wglu_ref[...], preferred_element_type=f32)
            out_ref[0:rows, :] = _rms(z * jax.nn.sigmoid(gl), gain_ref[...]).astype(out_ref.dtype)

    @pl.when(i < nt_prompt)
    def _():
        S = tt // SUBLANES
        gather_rows(S)
        in_proj(tt)
        zero = jnp.zeros((SUBLANES, LC), f32)
        fin = scan(S, [zero] * (2 * len(chunks)), store=False)

        row = lax.broadcasted_iota(jnp.int32, (SUBLANES, LC), 0)
        c_in = hfp_ref[0, s]
        entry = []
        c_out_r, c_out_i = [], []
        for c, lo in enumerate(chunks):
            qr = psr_ref[:, lo:lo + LC]
            qi = psi_ref[:, lo:lo + LC]
            er, ei = fin[2 * c], fin[2 * c + 1]
            for d in (1, 2, 4):
                keep = row >= d
                tr = jnp.where(keep, jnp.broadcast_to(qr[d - 1:d], (SUBLANES, LC)), 0.0)
                ti = jnp.where(keep, jnp.broadcast_to(qi[d - 1:d], (SUBLANES, LC)), 0.0)
                pr, pi_ = _cmul(tr, ti, pltpu.roll(er, d, axis=0), pltpu.roll(ei, d, axis=0))
                er, ei = er + pr, ei + pi_
            cr = c_in[:, lo:lo + LC]
            ci = c_in[:, W + lo:W + lo + LC]
            first = row == 0
            gr = jnp.where(first, 1.0, pltpu.roll(qr, 1, axis=0))
            gi = jnp.where(first, 0.0, pltpu.roll(qi, 1, axis=0))
            pr, pi_ = _cmul(gr, gi, cr, ci)
            entry.append(jnp.where(first, 0.0, pltpu.roll(er, 1, axis=0)) + pr)
            entry.append(jnp.where(first, 0.0, pltpu.roll(ei, 1, axis=0)) + pi_)
            pr, pi_ = _cmul(qr[SUBLANES - 1:SUBLANES], qi[SUBLANES - 1:SUBLANES], cr, ci)
            c_out_r.append(er[SUBLANES - 1:SUBLANES] + pr)
            c_out_i.append(ei[SUBLANES - 1:SUBLANES] + pi_)
        hfp_ref[0, s] = jnp.concatenate(c_out_r + c_out_i, axis=-1)

        scan(S, entry, store=True)
        finish(tt, S)

    @pl.when(i == nt_prompt)
    def _():
        rows = n_seq_s * seq_s
        gather_rows(seq_s)
        in_proj(rows)
        h0 = h0s_ref[s]
        x0 = []
        for lo in chunks:
            x0 += [h0[:, lo:lo + LC], h0[:, W + lo:W + lo + LC]]
        fin = scan(seq_s, x0, store=True)
        hfs_ref[s] = jnp.concatenate([fin[2 * c] for c in range(len(chunks))]
                                     + [fin[2 * c + 1] for c in range(len(chunks))], axis=-1)
        finish(rows, seq_s)


def _s5_group(proj, bexp, cexp, p1r, p1i, psr, psi, d, h0p, h0s, wglu, gain, l, n_prompt, seq_s, casts=()):
    M = proj.shape[0]
    _, n_sets, cw, w2 = bexp.shape
    W = w2 // 2
    half = cw // 2
    tt = SEQ_TILE
    nt_prompt = n_prompt // tt
    n_seq_s = h0s.shape[1]
    assert half == LANES and n_seq_s == SUBLANES and seq_s % 4 == 0 and (tt // SUBLANES) % 4 == 0
    assert n_prompt % tt == 0 and n_seq_s * seq_s <= tt and M == n_prompt + n_seq_s * seq_s
    s5w = n_sets * cw
    kern = functools.partial(_s5_kernel, n_sets=n_sets, lane_chunk=min(256, W), nt_prompt=nt_prompt,
                             n_seq_s=n_seq_s, seq_s=seq_s, n_cast=len(casts))

    cast_in, cast_out, cast_shapes = [], [], []
    for w, wl in casts:
        _, rows_w, cols_w = w.shape
        n_panels = cols_w // LANES
        assert cols_w % LANES == 0 and (nt_prompt + 1) * n_sets >= n_panels

        def panel(i, s, n_panels=n_panels):
            return jnp.minimum(i * n_sets + s, n_panels - 1)

        cast_in.append(pl.BlockSpec((None, rows_w, LANES), lambda i, s, wl=wl, panel=panel: (wl, 0, panel(i, s))))
        cast_out.append(pl.BlockSpec((None, rows_w, LANES), lambda i, s, panel=panel: (0, 0, panel(i, s))))
        cast_shapes.append(jax.ShapeDtypeStruct((1, rows_w, cols_w), bf16))

    return pl.pallas_call(
        kern,
        grid=(nt_prompt + 1, n_sets),
        in_specs=[
            pl.BlockSpec((tt, half), lambda i, s: (i, 2 * s)),
            pl.BlockSpec((tt, half), lambda i, s: (i, 2 * s + 1)),
            pl.BlockSpec((None, None, cw, w2), lambda i, s: (l, s, 0, 0)),
            pl.BlockSpec((None, None, w2, cw), lambda i, s: (l, s, 0, 0)),
            pl.BlockSpec((None, 1, W), lambda i, s: (l, 0, s)),
            pl.BlockSpec((None, 1, W), lambda i, s: (l, 0, s)),
            pl.BlockSpec((None, SUBLANES, W), lambda i, s: (l, 0, s)),
            pl.BlockSpec((None, SUBLANES, W), lambda i, s: (l, 0, s)),
            pl.BlockSpec((None, None, 1, cw), lambda i, s: (l, s, 0, 0)),
            pl.BlockSpec((1, n_sets, 1, w2), lambda i, s: (0, 0, 0, 0)),
            pl.BlockSpec((n_sets, n_seq_s, w2), lambda i, s: (0, 0, 0)),
            pl.BlockSpec((None, s5w, s5w), lambda i, s: (l, 0, 0), pipeline_mode=pl.Buffered(1)),
            pl.BlockSpec((1, s5w), lambda i, s: (0, 0)),
        ] + cast_in,
        out_specs=[
            pl.BlockSpec((tt, s5w), lambda i, s: (i, 0)),
            pl.BlockSpec((1, n_sets, 1, w2), lambda i, s: (0, 0, 0, 0)),
            pl.BlockSpec((n_sets, n_seq_s, w2), lambda i, s: (0, 0, 0)),
        ] + cast_out,
        out_shape=[
            jax.ShapeDtypeStruct((M, s5w), bf16),
            jax.ShapeDtypeStruct((1, n_sets, 1, w2), f32),
            jax.ShapeDtypeStruct((n_sets, n_seq_s, w2), f32),
        ] + cast_shapes,
        scratch_shapes=[
            pltpu.VMEM((tt, cw), f32),
            pltpu.VMEM((tt, w2), f32),
            pltpu.VMEM((tt, w2), bf16),
            pltpu.VMEM((2, SUBLANES, W), f32),
            pltpu.VMEM((2 * n_sets, tt, half), f32),
        ],
        compiler_params=_cparams(("arbitrary", "arbitrary")),
        name="s5_group",
    )(proj, proj, bexp, cexp, p1r, p1i, psr, psi, d, h0p, h0s, wglu, gain, *[w for w, _ in casts])


def _pair_levels(C):
    t = np.arange(C)[:, None]
    s = np.arange(C)[None, :]
    lvl = np.full((C, C), -1, np.int32)
    m, n = 1, 0
    while m < C:
        own = (t // (2 * m) == s // (2 * m)) & ((t // m) % 2 == 1) & ((s // m) % 2 == 0)
        lvl[own] = n
        m *= 2
        n += 1
    return lvl


def _cumsum_rows(x):
    C = x.shape[0]
    nb = C // SUBLANES
    x3 = x.reshape(nb, SUBLANES, x.shape[1])
    row = lax.broadcasted_iota(jnp.int32, (1, SUBLANES, x.shape[1]), 1)
    for d in (1, 2, 4):
        x3 = x3 + jnp.where(row >= d, pltpu.roll(x3, d, axis=1), 0.0)
    outs = [x3[0:1]]
    acc = x3[0:1, SUBLANES - 1:SUBLANES, :]
    for b in range(1, nb):
        outs.append(x3[b:b + 1] + acc)
        acc = acc + x3[b:b + 1, SUBLANES - 1:SUBLANES, :]
    return jnp.concatenate(outs, axis=0).reshape(C, x.shape[1])


def _dot_nt(a, b):
    return lax.dot_general(a, b, (((1,), (1,)), ((), ())), preferred_element_type=f32)


def _dot_tn(a, b):
    return lax.dot_general(a, b, (((0,), (0,)), ((), ())), preferred_element_type=f32)


def _hgrn_chunk(q, z, v, g, lb, gain, st, lvl):
    C, K = q.shape
    nb = C // SUBLANES
    qf = q * jax.nn.sigmoid(q)
    e = jnp.exp(-jnp.abs(z))
    r = 1.0 / (1.0 + e)
    er = e * r
    pos = z >= 0.0
    om = 1.0 - lb
    fc = jnp.maximum(lb + om * jnp.where(pos, r, er), F_TINY)
    logf = jnp.log(fc)
    kf = om * jnp.where(pos, er, r)
    bc = _cumsum_rows(logf)
    blast = bc[C - 1:C, :]
    vb = v.astype(bf16)

    o = _dot_nt((qf * jnp.exp(bc)).astype(bf16), st.astype(bf16))
    st_new = st * jnp.exp(blast) + _dot_tn(vb, (kf * jnp.exp(blast - bc)).astype(bf16))

    o = o + jnp.sum(qf * kf, axis=-1, keepdims=True) * v

    q3 = qf.reshape(nb, SUBLANES, K)
    k3 = kf.reshape(nb, SUBLANES, K)
    b3 = bc.reshape(nb, SUBLANES, K)
    sub = lax.broadcasted_iota(jnp.int32, (1, SUBLANES, K), 1)
    att = None
    m, n = 1, 0
    while m < C:
        if m == 1:
            upper = (sub & 1) == 1
            qm = jnp.where(upper, q3 * fc.reshape(nb, SUBLANES, K), 0.0).reshape(C, K)
            km = jnp.where(upper, 0.0, k3).reshape(C, K)
        elif m < SUBLANES:
            bref = jnp.broadcast_to(b3[:, m - 1:m, :], b3.shape)
            for base in range(2 * m, SUBLANES, 2 * m):
                bref = jnp.where(sub >= base, jnp.broadcast_to(b3[:, base + m - 1:base + m, :], b3.shape), bref)
            dec = jnp.exp(-jnp.abs(b3 - bref))
            upper = (sub & m) == m
            qm = jnp.where(upper, q3 * dec, 0.0).reshape(C, K)
            km = jnp.where(upper, 0.0, k3 * dec).reshape(C, K)
        else:
            nblk = C // (2 * m)
            b4 = bc.reshape(nblk, 2, m, K)
            dec = jnp.exp(-jnp.abs(b4 - b4[:, 0:1, m - 1:m, :]))
            zero = jnp.zeros((nblk, 1, m, K), f32)
            qm = jnp.concatenate([zero, qf.reshape(nblk, 2, m, K)[:, 1:2] * dec[:, 1:2]], axis=1).reshape(C, K)
            km = jnp.concatenate([kf.reshape(nblk, 2, m, K)[:, 0:1] * dec[:, 0:1], zero], axis=1).reshape(C, K)
        a = _dot_nt(qm.astype(bf16), km.astype(bf16))
        att = a if att is None else jnp.where(lvl == n, a, att)
        m *= 2
        n += 1
    att = jnp.where(lvl >= 0, att, 0.0)
    o = o + jnp.dot(att.astype(bf16), vb, preferred_element_type=f32)

    out = _rms(o, gain) * (g * jax.nn.sigmoid(g))
    return out, st_new


def _hgrn_kernel(q_ref, z_ref, v_ref, g_ref, lvlp_ref, lvls_ref, lb_ref, gain_ref, s0p_ref, s0s_ref,
                 wg_ref, wu_ref, wd_ref,
                 out_ref, sfp_ref, sfs_ref, wgb_ref, wub_ref, wdb_ref, st_ref,
                 *, heads, chunk_p, nt_prompt, n_seq_s, seq_s):
    i = pl.program_id(1)
    tt = q_ref.shape[0]
    K = LANES
    gain = gain_ref[...]
    n_cast = tt // chunk_p

    def cast_part(c):
        def part(n):
            return slice(c * n, (c + 1) * n) if isinstance(c, int) else pl.ds(pl.multiple_of(c * n, n), n)
        rows = part(wg_ref.shape[0] // n_cast)
        wgb_ref[rows, :] = wg_ref[rows, :].astype(bf16)
        wub_ref[rows, :] = wu_ref[rows, :].astype(bf16)
        rows = part(wd_ref.shape[0] // n_cast)
        wdb_ref[rows, :] = wd_ref[rows, :].astype(bf16)

    def chunk(rows, h, st, lvl):
        ls = slice(h * K, (h + 1) * K)
        out, st_new = _hgrn_chunk(q_ref[rows, ls], z_ref[rows, ls], v_ref[rows, ls], g_ref[rows, ls],
                                  lb_ref[:, ls], gain, st, lvl)
        out_ref[rows, ls] = out.astype(out_ref.dtype)
        return st_new

    @pl.when(i == 0)
    def _():
        for h in range(heads):
            st_ref[h] = s0p_ref[0, h].T

    @pl.when(i < nt_prompt)
    def _():
        def body(c, carry):
            rows = pl.ds(pl.multiple_of(c * chunk_p, chunk_p), chunk_p)
            lvl = lvlp_ref[...]
            for h in range(heads):
                st_ref[h] = chunk(rows, h, st_ref[h], lvl)
            cast_part(c)
            return carry

        lax.fori_loop(0, tt // chunk_p, body, 0, unroll=2)

    @pl.when(i == nt_prompt - 1)
    def _():
        for h in range(heads):
            sfp_ref[0, h] = st_ref[h].T

    @pl.when(i == nt_prompt)
    def _():
        lvl = lvls_ref[...]
        for b in range(n_seq_s):
            for h in range(heads):
                sfs_ref[b, h] = chunk(slice(b * seq_s, (b + 1) * seq_s), h, s0s_ref[b, h].T, lvl).T
        for c in range(n_cast):
            cast_part(c)


def _hgrn_group(proj, lbs, gain, s0p, s0s, wg, wu, wd, l, n_prompt, seq_s, col0):
    M = proj.shape[0]
    _, D, F = wg.shape
    n_panels = F // CAST_BLOCK
    n_seq_s, H, K, V = s0s.shape
    assert K == LANES and V == LANES and s0p.shape[0] == 1
    heads = HG_HEADS_PER_STEP
    tt = SEQ_TILE
    nt_prompt = n_prompt // tt
    chunk_p = min(HG_CHUNK, n_prompt)
    assert seq_s <= HG_CHUNK and n_prompt % tt == 0 and tt % chunk_p == 0 and n_seq_s * seq_s <= tt
    assert M == n_prompt + n_seq_s * seq_s
    cb = col0 // (heads * K)
    hb = H // heads
    kern = functools.partial(_hgrn_kernel, heads=heads, chunk_p=chunk_p, nt_prompt=nt_prompt,
                             n_seq_s=n_seq_s, seq_s=seq_s)

    def col_spec(k):
        return pl.BlockSpec((tt, heads * K), lambda hp, i, k=k: (i, cb + k * hb + hp))

    n_steps = nt_prompt + 1
    assert hb * n_steps >= n_panels and F % CAST_BLOCK == 0

    def panel(hp, i):
        return jnp.minimum(hp * n_steps + i, n_panels - 1)

    return pl.pallas_call(
        kern,
        grid=(hb, n_steps),
        in_specs=[
            col_spec(0), col_spec(1), col_spec(2), col_spec(3),
            pl.BlockSpec((chunk_p, chunk_p), lambda hp, i: (0, 0)),
            pl.BlockSpec((seq_s, seq_s), lambda hp, i: (0, 0)),
            pl.BlockSpec((1, heads * K), lambda hp, i: (0, hp)),
            pl.BlockSpec((1, V), lambda hp, i: (0, 0)),
            pl.BlockSpec((1, heads, K, V), lambda hp, i: (0, hp, 0, 0)),
            pl.BlockSpec((n_seq_s, heads, K, V), lambda hp, i: (0, hp, 0, 0), pipeline_mode=pl.Buffered(1)),
            pl.BlockSpec((None, D, CAST_BLOCK), lambda hp, i: (l, 0, panel(hp, i))),
            pl.BlockSpec((None, D, CAST_BLOCK), lambda hp, i: (l, 0, panel(hp, i))),
            pl.BlockSpec((None, CAST_BLOCK, D), lambda hp, i: (l, panel(hp, i), 0)),
        ],
        out_specs=[
            pl.BlockSpec((tt, heads * V), lambda hp, i: (i, hp)),
            pl.BlockSpec((1, heads, K, V), lambda hp, i: (0, hp, 0, 0)),
            pl.BlockSpec((n_seq_s, heads, K, V), lambda hp, i: (0, hp, 0, 0)),
            pl.BlockSpec((None, D, CAST_BLOCK), lambda hp, i: (0, 0, panel(hp, i))),
            pl.BlockSpec((None, D, CAST_BLOCK), lambda hp, i: (0, 0, panel(hp, i))),
            pl.BlockSpec((None, CAST_BLOCK, D), lambda hp, i: (0, panel(hp, i), 0)),
        ],
        out_shape=[
            jax.ShapeDtypeStruct((M, H * V), bf16),
            jax.ShapeDtypeStruct((1, H, K, V), f32),
            jax.ShapeDtypeStruct((n_seq_s, H, K, V), f32),
            jax.ShapeDtypeStruct((1, D, F), bf16),
            jax.ShapeDtypeStruct((1, D, F), bf16),
            jax.ShapeDtypeStruct((1, F, D), bf16),
        ],
        scratch_shapes=[pltpu.VMEM((heads, V, K), f32)],
        compiler_params=_cparams(("arbitrary", "arbitrary")),
        name="hgrn2_group",
    )(proj, proj, proj, proj, jnp.asarray(_pair_levels(chunk_p)), jnp.asarray(_pair_levels(seq_s)),
      lbs, gain, s0p, s0s, wg, wu, wd)


def _out_proj_kernel(x_ref, a_ref, b_ref, wa_ref, wb_ref, o_ref):
    acc = jnp.dot(a_ref[...], wa_ref[...], preferred_element_type=f32)
    acc = acc + jnp.dot(b_ref[...], wb_ref[...], preferred_element_type=f32)
    o_ref[...] = x_ref[...] + acc


def _out_proj(x, a, b, w, l, tm, tn):
    M, D = x.shape
    ka = a.shape[1]
    kb = b.shape[1]
    assert ka == kb
    return pl.pallas_call(
        _out_proj_kernel,
        grid=(M // tm, D // tn),
        in_specs=[
            pl.BlockSpec((tm, tn), lambda i, j: (i, j)),
            pl.BlockSpec((tm, ka), lambda i, j: (i, 0)),
            pl.BlockSpec((tm, kb), lambda i, j: (i, 0)),
            pl.BlockSpec((None, ka, tn), lambda i, j: (l, 0, j)),
            pl.BlockSpec((None, kb, tn), lambda i, j: (l, 1, j)),
        ],
        out_specs=pl.BlockSpec((tm, tn), lambda i, j: (i, j)),
        out_shape=jax.ShapeDtypeStruct((M, D), f32),
        compiler_params=_cparams(("parallel", "arbitrary")),
        name="out_proj",
    )(x, a, b, w, w)


def _ffn_kernel(x_ref, g_ref, wg_ref, wu_ref, wd_ref, gf_ref, o_ref, h_ref, *, final_norm):
    j = pl.program_id(1)

    tm, D = o_ref.shape

    def row_blocks(fn):
        def body(r, carry):
            fn(pl.ds(pl.multiple_of(r * ROW_BLOCK, ROW_BLOCK), ROW_BLOCK))
            return carry
        lax.fori_loop(0, tm // ROW_BLOCK, body, 0)

    @pl.when(j == 0)
    def _():
        def init(rows):
            x = x_ref[rows, :]
            h_ref[rows, :] = _rms(x, g_ref[...]).astype(bf16)
            o_ref[rows, :] = x
        row_blocks(init)

    h = h_ref[...]
    gate = jnp.dot(h, wg_ref[...], preferred_element_type=f32)
    up = jnp.dot(h, wu_ref[...], preferred_element_type=f32)
    act = (gate * jax.nn.sigmoid(gate) * up).astype(bf16)
    for n in range(0, D, COL_PANEL):
        o_ref[:, n:n + COL_PANEL] += jnp.dot(act, wd_ref[:, n:n + COL_PANEL], preferred_element_type=f32)

    if final_norm:
        @pl.when(j == pl.num_programs(1) - 1)
        def _():
            def close(rows):
                o_ref[rows, :] = _rms(o_ref[rows, :], gf_ref[...])
            row_blocks(close)


def _ffn(x, gain, wg, wu, wd, gain_final, final_norm, l, tm, tf):
    M, D = x.shape
    F = wg.shape[2]
    kern = functools.partial(_ffn_kernel, final_norm=final_norm)
    return pl.pallas_call(
        kern,
        grid=(M // tm, F // tf),
        in_specs=[
            pl.BlockSpec((tm, D), lambda i, j: (i, 0), pipeline_mode=pl.Buffered(1)),
            pl.BlockSpec((1, D), lambda i, j: (0, 0)),
            pl.BlockSpec((None, D, tf), lambda i, j: (l, 0, j)),
            pl.BlockSpec((None, D, tf), lambda i, j: (l, 0, j)),
            pl.BlockSpec((None, tf, D), lambda i, j: (l, j, 0)),
            pl.BlockSpec((1, D), lambda i, j: (0, 0)),
        ],
        out_specs=pl.BlockSpec((tm, D), lambda i, j: (i, 0)),
        out_shape=jax.ShapeDtypeStruct((M, D), f32),
        scratch_shapes=[pltpu.VMEM((tm, D), bf16)],
        compiler_params=_cparams(("parallel", "arbitrary")),
        name="swiglu_ffn",
    )(x, gain.reshape(1, D), wg, wu, wd, gain_final.reshape(1, D))


def _expand_s5_weights(bb_re, bb_im, c_re, c_im):
    L, G, P, H = bb_re.shape
    ns = G // S5_SET_GROUPS
    grp_of_row = np.arange(S5_SET_GROUPS * H) // H
    grp_of_col = np.arange(S5_SET_GROUPS * P) // P
    same_group = jnp.asarray(grp_of_row[:, None] == grp_of_col[None, :], f32)

    def expand_b(bb):
        t = bb.reshape(L, ns, S5_SET_GROUPS, P, H).transpose(0, 1, 2, 4, 3).reshape(L, ns, S5_SET_GROUPS * H, P)
        return jnp.tile(t, (1, 1, 1, S5_SET_GROUPS)) * same_group

    def expand_c(c):
        t = c.reshape(L, ns, S5_SET_GROUPS * H, P).transpose(0, 1, 3, 2)
        return jnp.tile(t, (1, 1, S5_SET_GROUPS, 1)) * same_group.T

    bexp = jnp.concatenate([expand_b(bb_re), expand_b(bb_im)], axis=-1).astype(bf16)
    cexp = jnp.concatenate([expand_c(c_re), -expand_c(c_im)], axis=-2).astype(bf16)
    return bexp, cexp


def kernel(x_prompt, x_sample, state_s5_re, state_s5_im, state_hgrn, w_in, w_out, norm_mix, norm_ffn, norm_final, s5_a_re, s5_a_im, s5_log_dt, s5_b_re, s5_b_im, s5_c_re, s5_c_im, s5_d, s5_w_glu, s5_out_norm, hg_lower_bounds, hg_out_norm, ffn_w_gate, ffn_w_up, ffn_w_down):
    L, G, P = s5_a_re.shape
    Hs = s5_b_re.shape[-1]
    _, _, HH, HK, HV = state_hgrn.shape
    PB, PT, D = x_prompt.shape
    SB, ST, _ = x_sample.shape
    assert PB == 1
    ns = G // S5_SET_GROUPS
    W = S5_SET_GROUPS * P
    s5w = G * Hs
    n_prompt = PB * PT
    M = n_prompt + SB * ST
    assert M % ROW_TILE == 0

    p1r, p1i, psr, psi, bb_re, bb_im, lbs = _prep(s5_a_re, s5_a_im, s5_log_dt, s5_b_re, s5_b_im, hg_lower_bounds)
    bexp, cexp = _expand_s5_weights(bb_re, bb_im, s5_c_re, s5_c_im)
    d_sets = s5_d.reshape(L, ns, 1, S5_SET_GROUPS * Hs)

    w_in_b = w_in[0:1].astype(bf16)
    w_glu_b = s5_w_glu.astype(bf16)

    zero_s5 = jnp.zeros((PB, ns, 1, 2 * W), f32)
    zero_hg = jnp.zeros((PB, HH, HK, HV), f32)

    x = jnp.concatenate([x_prompt.reshape(n_prompt, D), x_sample.reshape(SB * ST, D)], axis=0)
    p_re, p_im, p_hg, s_re, s_im, s_hg = [], [], [], [], [], []
    for l in range(L):
        h0s = jnp.concatenate([state_s5_re[l].reshape(SB, ns, W), state_s5_im[l].reshape(SB, ns, W)],
                              axis=-1).transpose(1, 0, 2)
        proj = _norm_matmul(x, norm_mix[l], w_in_b, 0, ROW_TILE, 1024)
        casts = [(w_out, l)] + ([(w_in, l + 1)] if l + 1 < L else [])
        s5_out, hfp, hfs, w_out_b, *w_in_next = _s5_group(
            proj, bexp, cexp, p1r, p1i, psr, psi, d_sets, zero_s5, h0s, w_glu_b,
            s5_out_norm[l].reshape(1, s5w), l, n_prompt, ST, casts)
        if w_in_next:
            w_in_b = w_in_next[0]
        hfs = hfs.transpose(1, 0, 2)
        hg_out, sfp, sfs, wg_b, wu_b, wd_b = _hgrn_group(
            proj, lbs[l].reshape(1, HH * HK), hg_out_norm[l].reshape(1, HV), zero_hg, state_hgrn[l],
            ffn_w_gate, ffn_w_up, ffn_w_down, l, n_prompt, ST, s5w)
        x = _out_proj(x, s5_out, hg_out, w_out_b, 0, ROW_TILE, 1024)
        x = _ffn(x, norm_ffn[l], wg_b, wu_b, wd_b, norm_final, l == L - 1, 0, ROW_TILE, 256)
        p_re.append(hfp[..., :W].reshape(PB, G, P))
        p_im.append(hfp[..., W:].reshape(PB, G, P))
        p_hg.append(sfp)
        s_re.append(hfs[..., :W].reshape(SB, G, P))
        s_im.append(hfs[..., W:].reshape(SB, G, P))
        s_hg.append(sfs)
    y_prompt = x[:n_prompt].reshape(PB, PT, D)
    y_sample = x[n_prompt:].reshape(SB, ST, D)
    return (y_prompt, y_sample, jnp.stack(p_re), jnp.stack(p_im), jnp.stack(p_hg),
            jnp.stack(s_re), jnp.stack(s_im), jnp.stack(s_hg))
```
